```python
import math
import jax, jax.numpy as jnp
from jax import lax
import numpy as np

D_MODEL = 1024
BATCH = 8
SEQ = 8192
DEPTH = 1
DEC_BATCH = 1
DEC_SEQ = 16384
PAST_LEN = 128

N_HEADS = 4
HEAD_DIM = 64
V_DIM = 2 * HEAD_DIM
ATTN_WIDTH = N_HEADS * 2 * HEAD_DIM
CONV_WIDTH = 512
CONV_KERNEL = 31
D_FF = 2816
NUM_BUCKETS = 32
MAX_DISTANCE = 128
Q_BLOCK = 128
ALPHA = (2.0 * DEPTH) ** 0.25
BETA = (8.0 * DEPTH) ** -0.25
LN_EPS = 1e-5
ATTN_SCALE = HEAD_DIM ** -0.5
COL_U = 2 * CONV_WIDTH
COL_Q = ATTN_WIDTH
COL_K = ATTN_WIDTH
COL_V = N_HEADS * V_DIM
COL_G = 2 * D_MODEL
PROJ_COLS = COL_U + COL_Q + COL_K + COL_V + COL_G

kernel_name = "hybrid_conv_diffattn_encoder"


def layer_norm(x, g, b):
    xf = x.astype(jnp.float32)
    mu = jnp.mean(xf, -1, keepdims=True)
    var = jnp.mean(jnp.square(xf - mu), -1, keepdims=True)
    y = (xf - mu) * lax.rsqrt(var + LN_EPS) * g.astype(jnp.float32) + b.astype(jnp.float32)
    return y.astype(x.dtype)


def rms_norm(x, g):
    xf = x.astype(jnp.float32)
    y = xf * lax.rsqrt(jnp.mean(jnp.square(xf), -1, keepdims=True) + LN_EPS) * g.astype(jnp.float32)
    return y.astype(x.dtype)


def swiglu_ffn(x, w_gu, w_down):
    h = x @ w_gu
    a, g = h[..., :D_FF], h[..., D_FF:]
    return (jax.nn.silu(a) * g) @ w_down


def t5_bucket(rel):
    nb = NUM_BUCKETS // 2
    ret = jnp.where(rel > 0, nb, 0)
    n = jnp.abs(rel)
    max_exact = nb // 2
    nf = jnp.maximum(n, 1).astype(jnp.float32)
    large = max_exact + (jnp.log(nf / max_exact) / math.log(MAX_DISTANCE / max_exact)
                         * (nb - max_exact)).astype(jnp.int32)
    large = jnp.minimum(large, nb - 1)
    return ret + jnp.where(n < max_exact, n, large)


def conv_branch(u, w_dw, b_dw, ln_g, ln_b, w_out):
    h = u[..., :CONV_WIDTH] * jax.nn.sigmoid(u[..., CONV_WIDTH:])
    h = lax.conv_general_dilated(h, w_dw, window_strides=(1,),
                                 padding=((CONV_KERNEL // 2, CONV_KERNEL // 2),),
                                 dimension_numbers=('NWC', 'WIO', 'NWC'),
                                 feature_group_count=CONV_WIDTH) + b_dw
    h = jax.nn.silu(layer_norm(h, ln_g, ln_b))
    return h @ w_out


def diff_attention(q, k, v, lam, table):
    B, S = q.shape[0], q.shape[1]
    nblk = S // Q_BLOCK
    qb = q.reshape(B, nblk, Q_BLOCK, N_HEADS, 2, HEAD_DIM).transpose(1, 0, 2, 3, 4, 5)
    k_pos = jnp.arange(S, dtype=jnp.int32)

    def block(args):
        qi, idx = args
        q_pos = idx * Q_BLOCK + jnp.arange(Q_BLOCK, dtype=jnp.int32)
        bucket = t5_bucket(k_pos[None, :] - q_pos[:, None])
        bias = jnp.transpose(table[bucket], (2, 0, 1)).astype(jnp.float32)
        logits = jnp.einsum('bqhmd,bkhmd->bmhqk', qi, k).astype(jnp.float32) * ATTN_SCALE
        p = jax.nn.softmax(logits + bias[None, None], axis=-1)
        a = p[:, 0] - lam * p[:, 1]
        return jnp.einsum('bhqk,bkhe->bqhe', a.astype(v.dtype), v)

    out = lax.map(block, (qb, jnp.arange(nblk, dtype=jnp.int32)))
    return out.transpose(1, 0, 2, 3, 4).reshape(B, S, N_HEADS, V_DIM)


def mixer(x, table, w_in, b_gate, conv_w_dw, conv_b_dw, conv_ln_g, conv_ln_b, w_conv_out,
          lq1, lk1, lq2, lk2, subln_g, w_attn_out, w_o, layer_idx):
    B, S = x.shape[0], x.shape[1]
    p = x @ w_in
    o = 0
    u = p[..., o:o + COL_U]; o += COL_U
    q = p[..., o:o + COL_Q].reshape(B, S, N_HEADS, 2, HEAD_DIM); o += COL_Q
    k = p[..., o:o + COL_K].reshape(B, S, N_HEADS, 2, HEAD_DIM); o += COL_K
    v = p[..., o:o + COL_V].reshape(B, S, N_HEADS, V_DIM); o += COL_V
    gates = jax.nn.sigmoid(p[..., o:o + COL_G] + b_gate)
    g_conv, g_attn = gates[..., :D_MODEL], gates[..., D_MODEL:]

    y_conv = conv_branch(u, conv_w_dw, conv_b_dw, conv_ln_g, conv_ln_b, w_conv_out)

    lambda_init = 0.8 - 0.6 * math.exp(-0.3 * layer_idx)
    f32 = jnp.float32
    lam = (jnp.exp(jnp.sum(lq1.astype(f32) * lk1.astype(f32)))
           - jnp.exp(jnp.sum(lq2.astype(f32) * lk2.astype(f32))) + lambda_init)
    att = diff_attention(q, k, v, lam, table)
    att = rms_norm(att, subln_g) * (1.0 - lambda_init)
    y_attn = att.reshape(B, S, N_HEADS * V_DIM) @ w_attn_out

    merged = g_conv * y_conv + g_attn * y_attn
    return merged @ w_o


def trunk(x, rel_bias_table, ffn1_w_gu, ffn1_w_down, ln1_g, ln1_b, w_in, b_gate,
          conv_w_dw, conv_b_dw, conv_ln_g, conv_ln_b, w_conv_out,
          lambda_q1, lambda_k1, lambda_q2, lambda_k2, subln_g, w_attn_out, w_o,
          ln2_g, ln2_b, ffn2_w_gu, ffn2_w_down, ln3_g, ln3_b):
    for l in range(DEPTH):
        x = layer_norm(ALPHA * x + 0.5 * swiglu_ffn(x, ffn1_w_gu[l], ffn1_w_down[l]), ln1_g[l], ln1_b[l])
        m = mixer(x, rel_bias_table, w_in[l], b_gate[l], conv_w_dw[l], conv_b_dw[l],
                  conv_ln_g[l], conv_ln_b[l], w_conv_out[l], lambda_q1[l], lambda_k1[l],
                  lambda_q2[l], lambda_k2[l], subln_g[l], w_attn_out[l], w_o[l], l)
        x = layer_norm(ALPHA * x + m, ln2_g[l], ln2_b[l])
        x = layer_norm(ALPHA * x + 0.5 * swiglu_ffn(x, ffn2_w_gu[l], ffn2_w_down[l]), ln3_g[l], ln3_b[l])
    return x


def setup_inputs(seed: int = 0) -> dict:
    key = jax.random.key(seed)
    ks = iter(jax.random.split(key, 40))

    def nrm(shape, scale):
        return jax.random.normal(next(ks), shape, jnp.float32) * scale

    def gain(shape):
        return 1.0 + nrm(shape, 0.02)

    L, D = DEPTH, D_MODEL
    x_prompt = nrm((BATCH, SEQ, D), 1.0)
    x_sample = nrm((DEC_BATCH, DEC_SEQ, D), 1.0)
    rel_bias_table = nrm((NUM_BUCKETS, N_HEADS), 0.1)
    ffn1_w_gu = nrm((L, D, 2 * D_FF), D ** -0.5)
    ffn1_w_down = nrm((L, D_FF, D), D_FF ** -0.5 * BETA)
    ln1_g = gain((L, D)); ln1_b = nrm((L, D), 0.02)
    w_uqk = nrm((L, D, COL_U + COL_Q + COL_K), D ** -0.5)
    w_v = nrm((L, D, COL_V), D ** -0.5 * BETA)
    w_g = nrm((L, D, COL_G), D ** -0.5)
    w_in = jnp.concatenate([w_uqk, w_v, w_g], axis=-1)
    b_gate = nrm((L, COL_G), 0.02)
    conv_w_dw = nrm((L, CONV_KERNEL, 1, CONV_WIDTH), CONV_KERNEL ** -0.5)
    conv_b_dw = nrm((L, CONV_WIDTH), 0.02)
    conv_ln_g = gain((L, CONV_WIDTH)); conv_ln_b = nrm((L, CONV_WIDTH), 0.02)
    w_conv_out = nrm((L, CONV_WIDTH, D), CONV_WIDTH ** -0.5 * BETA)
    lambda_q1 = nrm((L, HEAD_DIM), 0.1)
    lambda_k1 = nrm((L, HEAD_DIM), 0.1)
    lambda_q2 = nrm((L, HEAD_DIM), 0.1)
    lambda_k2 = nrm((L, HEAD_DIM), 0.1)
    subln_g = gain((L, V_DIM))
    w_attn_out = nrm((L, N_HEADS * V_DIM, D), (N_HEADS * V_DIM) ** -0.5 * BETA)
    w_o = nrm((L, D, D), D ** -0.5 * BETA)
    ln2_g = gain((L, D)); ln2_b = nrm((L, D), 0.02)
    ffn2_w_gu = nrm((L, D, 2 * D_FF), D ** -0.5)
    ffn2_w_down = nrm((L, D_FF, D), D_FF ** -0.5 * BETA)
    ln3_g = gain((L, D)); ln3_b = nrm((L, D), 0.02)
    return {"x_prompt": x_prompt, "x_sample": x_sample, "rel_bias_table": rel_bias_table,
            "ffn1_w_gu": ffn1_w_gu, "ffn1_w_down": ffn1_w_down, "ln1_g": ln1_g, "ln1_b": ln1_b,
            "w_in": w_in, "b_gate": b_gate, "conv_w_dw": conv_w_dw, "conv_b_dw": conv_b_dw,
            "conv_ln_g": conv_ln_g, "conv_ln_b": conv_ln_b, "w_conv_out": w_conv_out,
            "lambda_q1": lambda_q1, "lambda_k1": lambda_k1, "lambda_q2": lambda_q2,
            "lambda_k2": lambda_k2, "subln_g": subln_g, "w_attn_out": w_attn_out, "w_o": w_o,
            "ln2_g": ln2_g, "ln2_b": ln2_b, "ffn2_w_gu": ffn2_w_gu, "ffn2_w_down": ffn2_w_down,
            "ln3_g": ln3_g, "ln3_b": ln3_b}


def reference(x_prompt, x_sample, rel_bias_table, ffn1_w_gu, ffn1_w_down, ln1_g, ln1_b,
              w_in, b_gate, conv_w_dw, conv_b_dw, conv_ln_g, conv_ln_b, w_conv_out,
              lambda_q1, lambda_k1, lambda_q2, lambda_k2, subln_g, w_attn_out, w_o,
              ln2_g, ln2_b, ffn2_w_gu, ffn2_w_down, ln3_g, ln3_b):
    y_prompt = trunk(x_prompt, rel_bias_table, ffn1_w_gu, ffn1_w_down, ln1_g, ln1_b, w_in, b_gate,
                     conv_w_dw, conv_b_dw, conv_ln_g, conv_ln_b, w_conv_out,
                     lambda_q1, lambda_k1, lambda_q2, lambda_k2, subln_g, w_attn_out, w_o,
                     ln2_g, ln2_b, ffn2_w_gu, ffn2_w_down, ln3_g, ln3_b)
    y_sample = trunk(x_sample, rel_bias_table, ffn1_w_gu, ffn1_w_down, ln1_g, ln1_b, w_in, b_gate,
                     conv_w_dw, conv_b_dw, conv_ln_g, conv_ln_b, w_conv_out,
                     lambda_q1, lambda_k1, lambda_q2, lambda_k2, subln_g, w_attn_out, w_o,
                     ln2_g, ln2_b, ffn2_w_gu, ffn2_w_down, ln3_g, ln3_b)
    return (y_prompt, y_sample)
```

```python
import functools
import math

import jax
import jax.numpy as jnp
from jax import lax
from jax.experimental import pallas as pl
from jax.experimental.pallas import tpu as pltpu

F32 = jnp.float32
BF16 = jnp.bfloat16

D_MODEL = 1024
N_HEADS = 4
HEAD_DIM = 64
V_DIM = 2 * HEAD_DIM
ATTN_WIDTH = N_HEADS * 2 * HEAD_DIM
CONV_WIDTH = 512
CONV_KERNEL = 31
CONV_HALO = 16
D_FF = 2816
NUM_BUCKETS = 32
MAX_DISTANCE = 128
DEPTH = 1
ALPHA = (2.0 * DEPTH) ** 0.25
LN_EPS = 1e-5
ATTN_SCALE = HEAD_DIM ** -0.5
LAMBDA_INIT = 0.8 - 0.6 * math.exp(-0.3 * 0)
COL_U = 2 * CONV_WIDTH
COL_Q = ATTN_WIDTH
COL_K = ATTN_WIDTH
COL_V = N_HEADS * V_DIM

V7X_VMEM_LIMIT_BYTES = 56 * 1024 * 1024
FF_CHUNK = 256
TOKEN_TILE = 1024
CONV_TILE = 256
CONV_ROWS = 64
ATTN_TILE = 512


def _dot(a, b):
    return jnp.dot(a, b, preferred_element_type=F32)


def _layer_norm(z, g, b):
    mu = jnp.mean(z, axis=-1, keepdims=True)
    zc = z - mu
    var = jnp.mean(zc * zc, axis=-1, keepdims=True)
    return zc * lax.rsqrt(var + LN_EPS) * g + b


def _resident(shape):
    return pl.BlockSpec(shape, lambda *_: (0,) * len(shape), pipeline_mode=pl.Buffered(1))


def _params(*semantics):
    return pltpu.CompilerParams(dimension_semantics=semantics,
                                vmem_limit_bytes=V7X_VMEM_LIMIT_BYTES)


def _ffn_ln_kernel(x_ref, wg_ref, wu_ref, wd_ref, g_ref, b_ref, o_ref, h_ref):
    x = x_ref[...]
    xb = x.astype(BF16)
    for c in range(D_FF // FF_CHUNK):
        sl = slice(c * FF_CHUNK, (c + 1) * FF_CHUNK)
        a = _dot(xb, wg_ref[:, sl])
        u = _dot(xb, wu_ref[:, sl])
        h_ref[:, sl] = (a * jax.nn.sigmoid(a) * u).astype(BF16)
    y = _dot(h_ref[...], wd_ref[...])
    o_ref[...] = _layer_norm(ALPHA * x + 0.5 * y, g_ref[...], b_ref[...])


def _ffn_ln(x, wg, wu, wd, g, b):
    t = x.shape[0]
    tm = min(TOKEN_TILE, t)
    row = pl.BlockSpec((tm, D_MODEL), lambda i: (i, 0))
    return pl.pallas_call(
        _ffn_ln_kernel,
        grid=(t // tm,),
        in_specs=[row, _resident((D_MODEL, D_FF)), _resident((D_MODEL, D_FF)),
                  _resident((D_FF, D_MODEL)), _resident((1, D_MODEL)), _resident((1, D_MODEL))],
        out_specs=row,
        out_shape=jax.ShapeDtypeStruct((t, D_MODEL), F32),
        scratch_shapes=[pltpu.VMEM((tm, D_FF), BF16)],
        compiler_params=_params("parallel"),
        name="ffn_ln",
    )(x, wg, wu, wd, g, b)


def _in_proj_kernel(x_ref, w_ref, hg_ref, q_ref, kt_ref, v_ref):
    xb = x_ref[...].astype(BF16)
    u = _dot(xb, w_ref[:, 0:COL_U])
    hg_ref[...] = u[:, :CONV_WIDTH] * jax.nn.sigmoid(u[:, CONV_WIDTH:])
    o = COL_U
    q_ref[...] = (_dot(xb, w_ref[:, o:o + COL_Q]) * ATTN_SCALE).astype(BF16)
    o += COL_Q
    kt_ref[0] = _dot(xb, w_ref[:, o:o + COL_K]).T.astype(BF16)
    o += COL_K
    v_ref[...] = _dot(xb, w_ref[:, o:o + COL_V]).astype(BF16)


def _in_proj(x1, w_uqkv, batch, seq):
    t = x1.shape[0]
    tm = min(TOKEN_TILE, seq)
    per_seq = seq // tm
    ncol = COL_U + COL_Q + COL_K + COL_V
    return pl.pallas_call(
        _in_proj_kernel,
        grid=(t // tm,),
        in_specs=[pl.BlockSpec((tm, D_MODEL), lambda i: (i, 0)), _resident((D_MODEL, ncol))],
        out_specs=[pl.BlockSpec((tm, CONV_WIDTH), lambda i: (i, 0)),
                   pl.BlockSpec((tm, ATTN_WIDTH), lambda i: (i, 0)),
                   pl.BlockSpec((1, ATTN_WIDTH, tm), lambda i: (i // per_seq, 0, i % per_seq)),
                   pl.BlockSpec((tm, ATTN_WIDTH), lambda i: (i, 0))],
        out_shape=[jax.ShapeDtypeStruct((t, CONV_WIDTH), F32),
                   jax.ShapeDtypeStruct((t, ATTN_WIDTH), BF16),
                   jax.ShapeDtypeStruct((batch, ATTN_WIDTH, seq), BF16),
                   jax.ShapeDtypeStruct((t, ATTN_WIDTH), BF16)],
        compiler_params=_params("parallel"),
        name="in_proj",
    )(x1, w_uqkv)


def _conv_kernel(prev_ref, cur_ref, next_ref, w_ref, b_ref, g_ref, beta_ref, o_ref, ext_ref, *, tc):
    i = pl.program_id(1)
    last = pl.num_programs(1) - 1
    ext_ref[0:CONV_HALO, :] = jnp.where(i > 0, prev_ref[0], 0.0)
    ext_ref[CONV_HALO:CONV_HALO + tc, :] = cur_ref[0]
    ext_ref[CONV_HALO + tc:2 * CONV_HALO + tc, :] = jnp.where(i < last, next_ref[0], 0.0)
    first_tap = CONV_HALO - CONV_KERNEL // 2
    for r in range(tc // CONV_ROWS):
        base = r * CONV_ROWS + first_tap
        acc = jnp.broadcast_to(b_ref[...], (CONV_ROWS, CONV_WIDTH))
        for j in range(CONV_KERNEL):
            acc = acc + w_ref[j:j + 1, :] * ext_ref[base + j:base + j + CONV_ROWS, :]
        y = _layer_norm(acc, g_ref[...], beta_ref[...])
        o_ref[0, r * CONV_ROWS:(r + 1) * CONV_ROWS, :] = (y * jax.nn.sigmoid(y)).astype(BF16)


def _conv_branch(hg, w_dw, b_dw, ln_g, ln_b, batch, seq):
    tc = min(CONV_TILE, seq)
    hpb = tc // CONV_HALO
    nhalo = seq // CONV_HALO
    hg3 = hg.reshape(batch, seq, CONV_WIDTH)
    out = pl.pallas_call(
        functools.partial(_conv_kernel, tc=tc),
        grid=(batch, seq // tc),
        in_specs=[
            pl.BlockSpec((1, CONV_HALO, CONV_WIDTH), lambda b, i: (b, jnp.maximum(i * hpb - 1, 0), 0)),
            pl.BlockSpec((1, tc, CONV_WIDTH), lambda b, i: (b, i, 0)),
            pl.BlockSpec((1, CONV_HALO, CONV_WIDTH),
                         lambda b, i: (b, jnp.minimum((i + 1) * hpb, nhalo - 1), 0)),
            _resident((CONV_KERNEL, CONV_WIDTH)), _resident((1, CONV_WIDTH)),
            _resident((1, CONV_WIDTH)), _resident((1, CONV_WIDTH))],
        out_specs=pl.BlockSpec((1, tc, CONV_WIDTH), lambda b, i: (b, i, 0)),
        out_shape=jax.ShapeDtypeStruct((batch, seq, CONV_WIDTH), BF16),
        scratch_shapes=[pltpu.VMEM((tc + 2 * CONV_HALO, CONV_WIDTH), F32)],
        compiler_params=_params("parallel", "parallel"),
        name="conv_branch",
    )(hg3, hg3, hg3, w_dw, b_dw, ln_g, ln_b)
    return out.reshape(batch * seq, CONV_WIDTH)


def _t5_bucket(rel):
    nb = NUM_BUCKETS // 2
    ret = jnp.where(rel > 0, nb, 0)
    n = jnp.abs(rel)
    max_exact = nb // 2
    nf = jnp.maximum(n, 1).astype(F32)
    large = max_exact + (jnp.log(nf / max_exact) / math.log(MAX_DISTANCE / max_exact)
                         * (nb - max_exact)).astype(jnp.int32)
    large = jnp.minimum(large, nb - 1)
    return ret + jnp.where(n < max_exact, n, large)


def _bias_kernel(tab_ref, o_ref, *, tile):
    h = pl.program_id(0)
    d = pl.program_id(1)
    row = lax.broadcasted_iota(jnp.int32, (tile, tile), 0)
    col = lax.broadcasted_iota(jnp.int32, (tile, tile), 1)
    bucket = _t5_bucket(col - row + (d - 1) * tile)
    acc = jnp.zeros((tile, tile), F32)
    for b in range(NUM_BUCKETS):
        acc = jnp.where(bucket == b, tab_ref[b, h], acc)
    o_ref[0, 0] = acc


def _bias_tiles(table, tile):
    return pl.pallas_call(
        functools.partial(_bias_kernel, tile=tile),
        grid=(N_HEADS, 3),
        in_specs=[pl.BlockSpec(memory_space=pltpu.SMEM)],
        out_specs=pl.BlockSpec((1, 1, tile, tile), lambda h, d: (h, d, 0, 0)),
        out_shape=jax.ShapeDtypeStruct((N_HEADS, 3, tile, tile), F32),
        compiler_params=_params("parallel", "parallel"),
        name="bias_tiles",
    )(table)


def _attn_kernel(tab_ref, q_ref, kt_ref, v_ref, bias_ref, lq1_ref, lk1_ref, lq2_ref, lk2_ref, g_ref,
                 o_ref, acc_ref, m_ref, *, tile, nk):
    h = pl.program_id(1)
    qi = pl.program_id(2)
    q = q_ref[...]
    lane = lax.broadcasted_iota(jnp.int32, q.shape, 1)
    zero = jnp.zeros_like(q)
    q_maps = (jnp.where(lane < HEAD_DIM, q, zero), jnp.where(lane >= HEAD_DIM, q, zero))
    ones = jnp.ones((tile, V_DIM), BF16)
    m_ref[...] = jnp.full(m_ref.shape, -jnp.inf, F32)
    acc_ref[...] = jnp.zeros(acc_ref.shape, F32)

    def step(ki, bias_fn):
        k0 = pl.multiple_of(ki * tile, tile)
        kt = kt_ref[0, :, pl.ds(k0, tile)]
        v_aug = jnp.concatenate([v_ref[pl.ds(k0, tile), :], ones], axis=1)
        for mi in range(2):
            s, shift = bias_fn(_dot(q_maps[mi], kt), ki)
            m_old = m_ref[mi]
            m_new = jnp.maximum(m_old, jnp.max(s, axis=-1, keepdims=True) + shift)
            p = jnp.exp(s - (m_new - shift))
            acc_ref[mi] = jnp.exp(m_old - m_new) * acc_ref[mi] + _dot(p.astype(BF16), v_aug)
            m_ref[mi] = m_new

    def far_left(ki, carry):
        step(ki, lambda s, _: (s, tab_ref[NUM_BUCKETS // 2 - 1, h]))
        return carry

    def far_right(ki, carry):
        step(ki, lambda s, _: (s, tab_ref[NUM_BUCKETS - 1, h]))
        return carry

    def near(ki, carry):
        step(ki, lambda s, k: (s + bias_ref[0, k - qi + 1], 0.0))
        return carry

    lo = jnp.maximum(qi - 1, 0)
    hi = jnp.minimum(qi + 2, nk)
    lax.fori_loop(0, lo, far_left, 0)
    lax.fori_loop(lo, hi, near, 0)
    lax.fori_loop(hi, nk, far_right, 0)

    lam = (jnp.exp(jnp.sum(lq1_ref[...] * lk1_ref[...], keepdims=True))
           - jnp.exp(jnp.sum(lq2_ref[...] * lk2_ref[...], keepdims=True)) + LAMBDA_INIT)
    o1 = acc_ref[0]
    o2 = acc_ref[1]
    a = o1[:, :V_DIM] / o1[:, V_DIM:] - lam * (o2[:, :V_DIM] / o2[:, V_DIM:])
    r = a * lax.rsqrt(jnp.mean(a * a, axis=-1, keepdims=True) + LN_EPS) * g_ref[...]
    o_ref[...] = (r * (1.0 - LAMBDA_INIT)).astype(BF16)


def _attention(q, kt, v, table, bias, lq1, lk1, lq2, lk2, subln_g, batch, seq, tile):
    nk = seq // tile
    small = _resident((1, HEAD_DIM))
    return pl.pallas_call(
        functools.partial(_attn_kernel, tile=tile, nk=nk),
        grid=(batch, N_HEADS, nk),
        in_specs=[
            pl.BlockSpec(memory_space=pltpu.SMEM),
            pl.BlockSpec((tile, V_DIM), lambda b, h, i: (b * nk + i, h)),
            pl.BlockSpec((1, V_DIM, seq), lambda b, h, i: (b, h, 0)),
            pl.BlockSpec((seq, V_DIM), lambda b, h, i: (b, h)),
            pl.BlockSpec((1, 3, tile, tile), lambda b, h, i: (h, 0, 0, 0)),
            small, small, small, small, _resident((1, V_DIM))],
        out_specs=pl.BlockSpec((tile, V_DIM), lambda b, h, i: (b * nk + i, h)),
        out_shape=jax.ShapeDtypeStruct((batch * seq, ATTN_WIDTH), BF16),
        scratch_shapes=[pltpu.VMEM((2, tile, 2 * V_DIM), F32), pltpu.VMEM((2, tile, 1), F32)],
        compiler_params=_params("parallel", "parallel", "parallel"),
        name="diff_attention",
    )(table, q, kt, v, bias, lq1, lk1, lq2, lk2, subln_g)


def _merge_ln_kernel(x_ref, hc_ref, att_ref, wco_ref, wao_ref, wg_ref, bg_ref, wo_ref, g_ref, b_ref, o_ref):
    x = x_ref[...]
    xb = x.astype(BF16)
    y_conv = _dot(hc_ref[...], wco_ref[...])
    y_attn = _dot(att_ref[...], wao_ref[...])
    g_conv = jax.nn.sigmoid(_dot(xb, wg_ref[:, :D_MODEL]) + bg_ref[:, :D_MODEL])
    g_attn = jax.nn.sigmoid(_dot(xb, wg_ref[:, D_MODEL:]) + bg_ref[:, D_MODEL:])
    merged = g_conv * y_conv + g_attn * y_attn
    m = _dot(merged.astype(BF16), wo_ref[...])
    o_ref[...] = _layer_norm(ALPHA * x + m, g_ref[...], b_ref[...])


def _merge_ln(x1, hc, att, wco, wao, wg, bg, wo, g, b):
    t = x1.shape[0]
    tm = min(TOKEN_TILE, t)
    row = pl.BlockSpec((tm, D_MODEL), lambda i: (i, 0))
    half = pl.BlockSpec((tm, CONV_WIDTH), lambda i: (i, 0))
    return pl.pallas_call(
        _merge_ln_kernel,
        grid=(t // tm,),
        in_specs=[row, half, half, _resident((CONV_WIDTH, D_MODEL)), _resident((ATTN_WIDTH, D_MODEL)),
                  _resident((D_MODEL, 2 * D_MODEL)), _resident((1, 2 * D_MODEL)),
                  _resident((D_MODEL, D_MODEL)), _resident((1, D_MODEL)), _resident((1, D_MODEL))],
        out_specs=row,
        out_shape=jax.ShapeDtypeStruct((t, D_MODEL), F32),
        compiler_params=_params("parallel"),
        name="merge_ln",
    )(x1, hc, att, wco, wao, wg, bg, wo, g, b)


def _layer(x, p, bias, tile):
    batch, seq, _ = x.shape
    x0 = x.reshape(batch * seq, D_MODEL)
    x1 = _ffn_ln(x0, p["ffn1_wg"], p["ffn1_wu"], p["ffn1_wd"], p["ln1_g"], p["ln1_b"])
    hg, q, kt, v = _in_proj(x1, p["w_uqkv"], batch, seq)
    hc = _conv_branch(hg, p["conv_w"], p["conv_b"], p["conv_ln_g"], p["conv_ln_b"], batch, seq)
    att = _attention(q, kt, v, p["table"], bias, p["lq1"], p["lk1"], p["lq2"], p["lk2"], p["subln_g"],
                     batch, seq, tile)
    x2 = _merge_ln(x1, hc, att, p["w_conv_out"], p["w_attn_out"], p["w_gate"], p["b_gate"], p["w_o"],
                   p["ln2_g"], p["ln2_b"])
    x3 = _ffn_ln(x2, p["ffn2_wg"], p["ffn2_wu"], p["ffn2_wd"], p["ln3_g"], p["ln3_b"])
    return x3.reshape(batch, seq, D_MODEL)


def _prepare(rel_bias_table, ffn1_w_gu, ffn1_w_down, ln1_g, ln1_b, w_in, b_gate, conv_w_dw, conv_b_dw,
             conv_ln_g, conv_ln_b, w_conv_out, lambda_q1, lambda_k1, lambda_q2, lambda_k2, subln_g,
             w_attn_out, w_o, ln2_g, ln2_b, ffn2_w_gu, ffn2_w_down, ln3_g, ln3_b):
    l = 0
    n_uqkv = COL_U + COL_Q + COL_K + COL_V
    row = lambda a: a[l].reshape(1, -1).astype(F32)
    return {
        "table": rel_bias_table.astype(F32),
        "ffn1_wg": ffn1_w_gu[l, :, :D_FF].astype(BF16), "ffn1_wu": ffn1_w_gu[l, :, D_FF:].astype(BF16),
        "ffn1_wd": ffn1_w_down[l].astype(BF16), "ln1_g": row(ln1_g), "ln1_b": row(ln1_b),
        "w_uqkv": w_in[l, :, :n_uqkv].astype(BF16), "w_gate": w_in[l, :, n_uqkv:].astype(BF16),
        "b_gate": row(b_gate),
        "conv_w": conv_w_dw[l].reshape(CONV_KERNEL, CONV_WIDTH).astype(F32), "conv_b": row(conv_b_dw),
        "conv_ln_g": row(conv_ln_g), "conv_ln_b": row(conv_ln_b),
        "w_conv_out": w_conv_out[l].astype(BF16),
        "lq1": row(lambda_q1), "lk1": row(lambda_k1), "lq2": row(lambda_q2), "lk2": row(lambda_k2),
        "subln_g": row(subln_g), "w_attn_out": w_attn_out[l].astype(BF16), "w_o": w_o[l].astype(BF16),
        "ln2_g": row(ln2_g), "ln2_b": row(ln2_b),
        "ffn2_wg": ffn2_w_gu[l, :, :D_FF].astype(BF16), "ffn2_wu": ffn2_w_gu[l, :, D_FF:].astype(BF16),
        "ffn2_wd": ffn2_w_down[l].astype(BF16), "ln3_g": row(ln3_g), "ln3_b": row(ln3_b),
    }


def kernel(x_prompt, x_sample, rel_bias_table, ffn1_w_gu, ffn1_w_down, ln1_g, ln1_b, w_in, b_gate, conv_w_dw, conv_b_dw, conv_ln_g, conv_ln_b, w_conv_out, lambda_q1, lambda_k1, lambda_q2, lambda_k2, subln_g, w_attn_out, w_o, ln2_g, ln2_b, ffn2_w_gu, ffn2_w_down, ln3_g, ln3_b):
    p = _prepare(rel_bias_table, ffn1_w_gu, ffn1_w_down, ln1_g, ln1_b, w_in, b_gate, conv_w_dw, conv_b_dw,
                 conv_ln_g, conv_ln_b, w_conv_out, lambda_q1, lambda_k1, lambda_q2, lambda_k2, subln_g,
                 w_attn_out, w_o, ln2_g, ln2_b, ffn2_w_gu, ffn2_w_down, ln3_g, ln3_b)
    tile = min(ATTN_TILE, x_prompt.shape[1], x_sample.shape[1])
    bias = _bias_tiles(p["table"], tile)
    return (_layer(x_prompt, p, bias, tile), _layer(x_sample, p, bias, tile))
```

```python
import functools
import math

import jax
import jax.numpy as jnp
from jax import lax
from jax.experimental import pallas as pl
from jax.experimental.pallas import tpu as pltpu

F32 = jnp.float32
BF16 = jnp.bfloat16

D_MODEL = 1024
N_HEADS = 4
HEAD_DIM = 64
V_DIM = 2 * HEAD_DIM
ATTN_WIDTH = N_HEADS * 2 * HEAD_DIM
CONV_WIDTH = 512
CONV_KERNEL = 31
CONV_HALO = 16
D_FF = 2816
NUM_BUCKETS = 32
MAX_DISTANCE = 128
DEPTH = 1
ALPHA = (2.0 * DEPTH) ** 0.25
LN_EPS = 1e-5
ATTN_SCALE = HEAD_DIM ** -0.5
LAMBDA_INIT = 0.8 - 0.6 * math.exp(-0.3 * 0)
LOG2E = math.log2(math.e)
ONES_ROWS = 16
COL_U = 2 * CONV_WIDTH
COL_Q = ATTN_WIDTH
COL_K = ATTN_WIDTH
COL_V = N_HEADS * V_DIM

V7X_VMEM_LIMIT_BYTES = 56 * 1024 * 1024
FF_CHUNK = 256
TOKEN_TILE = 1024
CONV_TILE = 256
CONV_ROWS = 64
ATTN_TILE = 512


def _dot(a, b):
    return jnp.dot(a, b, preferred_element_type=F32)


def _layer_norm(z, g, b):
    mu = jnp.mean(z, axis=-1, keepdims=True)
    zc = z - mu
    var = jnp.mean(zc * zc, axis=-1, keepdims=True)
    return zc * lax.rsqrt(var + LN_EPS) * g + b


def _resident(shape):
    return pl.BlockSpec(shape, lambda *_: (0,) * len(shape), pipeline_mode=pl.Buffered(1))


def _params(*semantics):
    return pltpu.CompilerParams(dimension_semantics=semantics,
                                vmem_limit_bytes=V7X_VMEM_LIMIT_BYTES)


def _ffn_ln_kernel(x_ref, wg_ref, wu_ref, wd_ref, g_ref, b_ref, o_ref, h_ref):
    x = x_ref[...]
    xb = x.astype(BF16)
    for c in range(D_FF // FF_CHUNK):
        sl = slice(c * FF_CHUNK, (c + 1) * FF_CHUNK)
        a = _dot(xb, wg_ref[:, sl])
        u = _dot(xb, wu_ref[:, sl])
        h_ref[:, sl] = (a * jax.nn.sigmoid(a) * u).astype(BF16)
    y = _dot(h_ref[...], wd_ref[...])
    o_ref[...] = _layer_norm(ALPHA * x + 0.5 * y, g_ref[...], b_ref[...])


def _ffn_ln(x, wg, wu, wd, g, b):
    t = x.shape[0]
    tm = min(TOKEN_TILE, t)
    row = pl.BlockSpec((tm, D_MODEL), lambda i: (i, 0))
    return pl.pallas_call(
        _ffn_ln_kernel,
        grid=(t // tm,),
        in_specs=[row, _resident((D_MODEL, D_FF)), _resident((D_MODEL, D_FF)),
                  _resident((D_FF, D_MODEL)), _resident((1, D_MODEL)), _resident((1, D_MODEL))],
        out_specs=row,
        out_shape=jax.ShapeDtypeStruct((t, D_MODEL), F32),
        scratch_shapes=[pltpu.VMEM((tm, D_FF), BF16)],
        compiler_params=_params("parallel"),
        name="ffn_ln",
    )(x, wg, wu, wd, g, b)


def _in_proj_kernel(x_ref, w_ref, hg_ref, qt_ref, k_ref, vt_ref):
    xb = x_ref[...].astype(BF16)
    u = _dot(xb, w_ref[:, 0:COL_U])
    hg_ref[...] = u[:, :CONV_WIDTH] * jax.nn.sigmoid(u[:, CONV_WIDTH:])
    o = COL_U
    qt_ref[0] = (_dot(xb, w_ref[:, o:o + COL_Q]) * (ATTN_SCALE * LOG2E)).T.astype(BF16)
    o += COL_Q
    k_ref[...] = _dot(xb, w_ref[:, o:o + COL_K]).astype(BF16)
    o += COL_K
    vt_ref[0] = _dot(xb, w_ref[:, o:o + COL_V]).T.astype(BF16)


def _in_proj(x1, w_uqkv, batch, seq):
    t = x1.shape[0]
    tm = min(TOKEN_TILE, seq)
    per_seq = seq // tm
    ncol = COL_U + COL_Q + COL_K + COL_V
    return pl.pallas_call(
        _in_proj_kernel,
        grid=(t // tm,),
        in_specs=[pl.BlockSpec((tm, D_MODEL), lambda i: (i, 0)), _resident((D_MODEL, ncol))],
        out_specs=[pl.BlockSpec((tm, CONV_WIDTH), lambda i: (i, 0)),
                   pl.BlockSpec((1, ATTN_WIDTH, tm), lambda i: (i // per_seq, 0, i % per_seq)),
                   pl.BlockSpec((tm, ATTN_WIDTH), lambda i: (i, 0)),
                   pl.BlockSpec((1, COL_V, tm), lambda i: (i // per_seq, 0, i % per_seq))],
        out_shape=[jax.ShapeDtypeStruct((t, CONV_WIDTH), F32),
                   jax.ShapeDtypeStruct((batch, ATTN_WIDTH, seq), BF16),
                   jax.ShapeDtypeStruct((t, ATTN_WIDTH), BF16),
                   jax.ShapeDtypeStruct((batch, COL_V, seq), BF16)],
        compiler_params=_params("parallel"),
        name="in_proj",
    )(x1, w_uqkv)


def _conv_kernel(prev_ref, cur_ref, next_ref, w_ref, b_ref, g_ref, beta_ref, o_ref, ext_ref, *, tc):
    i = pl.program_id(1)
    last = pl.num_programs(1) - 1
    ext_ref[0:CONV_HALO, :] = jnp.where(i > 0, prev_ref[0], 0.0)
    ext_ref[CONV_HALO:CONV_HALO + tc, :] = cur_ref[0]
    ext_ref[CONV_HALO + tc:2 * CONV_HALO + tc, :] = jnp.where(i < last, next_ref[0], 0.0)
    first_tap = CONV_HALO - CONV_KERNEL // 2
    for r in range(tc // CONV_ROWS):
        base = r * CONV_ROWS + first_tap
        acc = jnp.broadcast_to(b_ref[...], (CONV_ROWS, CONV_WIDTH))
        for j in range(CONV_KERNEL):
            acc = acc + w_ref[j:j + 1, :] * ext_ref[base + j:base + j + CONV_ROWS, :]
        y = _layer_norm(acc, g_ref[...], beta_ref[...])
        o_ref[0, r * CONV_ROWS:(r + 1) * CONV_ROWS, :] = (y * jax.nn.sigmoid(y)).astype(BF16)


def _conv_branch(hg, w_dw, b_dw, ln_g, ln_b, batch, seq):
    tc = min(CONV_TILE, seq)
    hpb = tc // CONV_HALO
    nhalo = seq // CONV_HALO
    hg3 = hg.reshape(batch, seq, CONV_WIDTH)
    out = pl.pallas_call(
        functools.partial(_conv_kernel, tc=tc),
        grid=(batch, seq // tc),
        in_specs=[
            pl.BlockSpec((1, CONV_HALO, CONV_WIDTH), lambda b, i: (b, jnp.maximum(i * hpb - 1, 0), 0)),
            pl.BlockSpec((1, tc, CONV_WIDTH), lambda b, i: (b, i, 0)),
            pl.BlockSpec((1, CONV_HALO, CONV_WIDTH),
                         lambda b, i: (b, jnp.minimum((i + 1) * hpb, nhalo - 1), 0)),
            _resident((CONV_KERNEL, CONV_WIDTH)), _resident((1, CONV_WIDTH)),
            _resident((1, CONV_WIDTH)), _resident((1, CONV_WIDTH))],
        out_specs=pl.BlockSpec((1, tc, CONV_WIDTH), lambda b, i: (b, i, 0)),
        out_shape=jax.ShapeDtypeStruct((batch, seq, CONV_WIDTH), BF16),
        scratch_shapes=[pltpu.VMEM((tc + 2 * CONV_HALO, CONV_WIDTH), F32)],
        compiler_params=_params("parallel", "parallel"),
        name="conv_branch",
    )(hg3, hg3, hg3, w_dw, b_dw, ln_g, ln_b)
    return out.reshape(batch * seq, CONV_WIDTH)


def _t5_bucket(rel):
    nb = NUM_BUCKETS // 2
    ret = jnp.where(rel > 0, nb, 0)
    n = jnp.abs(rel)
    max_exact = nb // 2
    nf = jnp.maximum(n, 1).astype(F32)
    large = max_exact + (jnp.log(nf / max_exact) / math.log(MAX_DISTANCE / max_exact)
                         * (nb - max_exact)).astype(jnp.int32)
    large = jnp.minimum(large, nb - 1)
    return ret + jnp.where(n < max_exact, n, large)


def _bias_kernel(tab_ref, o_ref, *, tile):
    h = pl.program_id(0)
    d = pl.program_id(1)
    key = lax.broadcasted_iota(jnp.int32, (tile, tile), 0)
    query = lax.broadcasted_iota(jnp.int32, (tile, tile), 1)
    bucket = _t5_bucket(key - query + (d - 1) * tile)
    acc = jnp.zeros((tile, tile), F32)
    for b in range(NUM_BUCKETS):
        acc = jnp.where(bucket == b, tab_ref[b, h], acc)
    o_ref[0, 0] = acc * LOG2E


def _bias_tiles(table, tile):
    return pl.pallas_call(
        functools.partial(_bias_kernel, tile=tile),
        grid=(N_HEADS, 3),
        in_specs=[pl.BlockSpec(memory_space=pltpu.SMEM)],
        out_specs=pl.BlockSpec((1, 1, tile, tile), lambda h, d: (h, d, 0, 0)),
        out_shape=jax.ShapeDtypeStruct((N_HEADS, 3, tile, tile), F32),
        compiler_params=_params("parallel", "parallel"),
        name="bias_tiles",
    )(table)


def _attn_kernel(tab_ref, qt_ref, k_ref, vt_ref, bias_ref, lq1_ref, lk1_ref, lq2_ref, lk2_ref, g_ref,
                 o_ref, acc_ref, m_ref, s_ref, mx_ref, *, tile, nk):
    h = pl.program_id(1)
    qi = pl.program_id(2)
    qt = qt_ref[0]
    row = lax.broadcasted_iota(jnp.int32, qt.shape, 0)
    zero = jnp.zeros_like(qt)
    qt_maps = (jnp.where(row < HEAD_DIM, qt, zero), jnp.where(row >= HEAD_DIM, qt, zero))
    ones = jnp.ones((ONES_ROWS, tile), BF16)
    m_ref[...] = jnp.full(m_ref.shape, -jnp.inf, F32)
    acc_ref[...] = jnp.zeros(acc_ref.shape, F32)

    def scores(ki, mi, bias):
        k0 = pl.multiple_of(ki * tile, tile)
        s = _dot(k_ref[pl.ds(k0, tile), :], qt_maps[mi])
        if bias is not None:
            s = s + bias
        s_ref[mi] = s
        mx_ref[mi] = jnp.max(s, axis=0, keepdims=True)

    def absorb(ki, mi, shift):
        k0 = pl.multiple_of(ki * tile, tile)
        m_old = m_ref[mi]
        m_new = jnp.maximum(m_old, mx_ref[mi] + shift)
        p = jnp.exp2(s_ref[mi] - (m_new - shift)).astype(BF16)
        vt_aug = jnp.concatenate([vt_ref[0, :, pl.ds(k0, tile)], ones], axis=0)
        acc_ref[mi] = jnp.exp2(m_old - m_new) * acc_ref[mi] + _dot(vt_aug, p)
        m_ref[mi] = m_new

    lo = jnp.maximum(qi - 1, 0)
    hi = jnp.minimum(qi + 2, nk)
    n_near = hi - lo
    n_far = nk - n_near
    near_bias = lambda ki: bias_ref[0, ki - qi + 1]
    far_tile = lambda j: jnp.where(j < lo, j, j + n_near)
    far_shift = lambda ki: LOG2E * jnp.where(ki < qi, tab_ref[NUM_BUCKETS // 2 - 1, h],
                                             tab_ref[NUM_BUCKETS - 1, h])

    def block(cur, cur_bias, cur_shift, nxt, nxt_bias):
        scores(cur, 1, cur_bias)
        absorb(cur, 0, cur_shift)
        if nxt is not None:
            scores(nxt, 0, nxt_bias)
        absorb(cur, 1, cur_shift)

    scores(lo, 0, near_bias(lo))

    def near(ki, carry):
        block(ki, near_bias(ki), 0.0, ki + 1, near_bias(ki + 1))
        return carry

    lax.fori_loop(lo, hi - 1, near, 0)
    block(hi - 1, near_bias(hi - 1), 0.0, far_tile(0), None)

    def far(j, carry):
        ki = far_tile(j)
        block(ki, None, far_shift(ki), far_tile(j + 1), None)
        return carry

    lax.fori_loop(0, n_far - 1, far, 0)
    last = far_tile(n_far - 1)
    block(last, None, far_shift(last), None, None)

    lam = (jnp.exp(jnp.sum(lq1_ref[...] * lk1_ref[...], keepdims=True))
           - jnp.exp(jnp.sum(lq2_ref[...] * lk2_ref[...], keepdims=True)) + LAMBDA_INIT)
    o1 = acc_ref[0]
    o2 = acc_ref[1]
    a = o1[:V_DIM] / o1[V_DIM:V_DIM + 1] - lam * (o2[:V_DIM] / o2[V_DIM:V_DIM + 1])
    r = a * lax.rsqrt(jnp.mean(a * a, axis=0, keepdims=True) + LN_EPS) * g_ref[...]
    o_ref[...] = (r * (1.0 - LAMBDA_INIT)).T.astype(BF16)


def _attention(qt, k, vt, table, bias, lq1, lk1, lq2, lk2, subln_g, batch, seq, tile):
    nk = seq // tile
    assert nk >= 4, "the far-tile pipeline needs at least one tile beyond the three near ones"
    small = _resident((1, HEAD_DIM))
    return pl.pallas_call(
        functools.partial(_attn_kernel, tile=tile, nk=nk),
        grid=(batch, N_HEADS, nk),
        in_specs=[
            pl.BlockSpec(memory_space=pltpu.SMEM),
            pl.BlockSpec((1, V_DIM, tile), lambda b, h, i: (b, h, i)),
            pl.BlockSpec((seq, V_DIM), lambda b, h, i: (b, h)),
            pl.BlockSpec((1, V_DIM, seq), lambda b, h, i: (b, h, 0)),
            pl.BlockSpec((1, 3, tile, tile), lambda b, h, i: (h, 0, 0, 0)),
            small, small, small, small, _resident((V_DIM, 1))],
        out_specs=pl.BlockSpec((tile, V_DIM), lambda b, h, i: (b * nk + i, h)),
        out_shape=jax.ShapeDtypeStruct((batch * seq, ATTN_WIDTH), BF16),
        scratch_shapes=[pltpu.VMEM((2, V_DIM + ONES_ROWS, tile), F32), pltpu.VMEM((2, 1, tile), F32),
                        pltpu.VMEM((2, tile, tile), F32), pltpu.VMEM((2, 1, tile), F32)],
        compiler_params=_params("parallel", "parallel", "parallel"),
        name="diff_attention",
    )(table, qt, k, vt, bias, lq1, lk1, lq2, lk2, subln_g)


def _merge_ln_kernel(x_ref, hc_ref, att_ref, wco_ref, wao_ref, wg_ref, bg_ref, wo_ref, g_ref, b_ref, o_ref):
    x = x_ref[...]
    xb = x.astype(BF16)
    y_conv = _dot(hc_ref[...], wco_ref[...])
    y_attn = _dot(att_ref[...], wao_ref[...])
    g_conv = jax.nn.sigmoid(_dot(xb, wg_ref[:, :D_MODEL]) + bg_ref[:, :D_MODEL])
    g_attn = jax.nn.sigmoid(_dot(xb, wg_ref[:, D_MODEL:]) + bg_ref[:, D_MODEL:])
    merged = g_conv * y_conv + g_attn * y_attn
    m = _dot(merged.astype(BF16), wo_ref[...])
    o_ref[...] = _layer_norm(ALPHA * x + m, g_ref[...], b_ref[...])


def _merge_ln(x1, hc, att, wco, wao, wg, bg, wo, g, b):
    t = x1.shape[0]
    tm = min(TOKEN_TILE, t)
    row = pl.BlockSpec((tm, D_MODEL), lambda i: (i, 0))
    half = pl.BlockSpec((tm, CONV_WIDTH), lambda i: (i, 0))
    return pl.pallas_call(
        _merge_ln_kernel,
        grid=(t // tm,),
        in_specs=[row, half, half, _resident((CONV_WIDTH, D_MODEL)), _resident((ATTN_WIDTH, D_MODEL)),
                  _resident((D_MODEL, 2 * D_MODEL)), _resident((1, 2 * D_MODEL)),
                  _resident((D_MODEL, D_MODEL)), _resident((1, D_MODEL)), _resident((1, D_MODEL))],
        out_specs=row,
        out_shape=jax.ShapeDtypeStruct((t, D_MODEL), F32),
        compiler_params=_params("parallel"),
        name="merge_ln",
    )(x1, hc, att, wco, wao, wg, bg, wo, g, b)


def _layer(x, p, bias, tile):
    batch, seq, _ = x.shape
    x0 = x.reshape(batch * seq, D_MODEL)
    x1 = _ffn_ln(x0, p["ffn1_wg"], p["ffn1_wu"], p["ffn1_wd"], p["ln1_g"], p["ln1_b"])
    hg, qt, k, vt = _in_proj(x1, p["w_uqkv"], batch, seq)
    hc = _conv_branch(hg, p["conv_w"], p["conv_b"], p["conv_ln_g"], p["conv_ln_b"], batch, seq)
    att = _attention(qt, k, vt, p["table"], bias, p["lq1"], p["lk1"], p["lq2"], p["lk2"], p["subln_g"],
                     batch, seq, tile)
    x2 = _merge_ln(x1, hc, att, p["w_conv_out"], p["w_attn_out"], p["w_gate"], p["b_gate"], p["w_o"],
                   p["ln2_g"], p["ln2_b"])
    x3 = _ffn_ln(x2, p["ffn2_wg"], p["ffn2_wu"], p["ffn2_wd"], p["ln3_g"], p["ln3_b"])
    return x3.reshape(batch, seq, D_MODEL)


def _prepare(rel_bias_table, ffn1_w_gu, ffn1_w_down, ln1_g, ln1_b, w_in, b_gate, conv_w_dw, conv_b_dw,
             conv_ln_g, conv_ln_b, w_conv_out, lambda_q1, lambda_k1, lambda_q2, lambda_k2, subln_g,
             w_attn_out, w_o, ln2_g, ln2_b, ffn2_w_gu, ffn2_w_down, ln3_g, ln3_b):
    l = 0
    n_uqkv = COL_U + COL_Q + COL_K + COL_V
    row = lambda a: a[l].reshape(1, -1).astype(F32)
    return {
        "table": rel_bias_table.astype(F32),
        "ffn1_wg": ffn1_w_gu[l, :, :D_FF].astype(BF16), "ffn1_wu": ffn1_w_gu[l, :, D_FF:].astype(BF16),
        "ffn1_wd": ffn1_w_down[l].astype(BF16), "ln1_g": row(ln1_g), "ln1_b": row(ln1_b),
        "w_uqkv": w_in[l, :, :n_uqkv].astype(BF16), "w_gate": w_in[l, :, n_uqkv:].astype(BF16),
        "b_gate": row(b_gate),
        "conv_w": conv_w_dw[l].reshape(CONV_KERNEL, CONV_WIDTH).astype(F32), "conv_b": row(conv_b_dw),
        "conv_ln_g": row(conv_ln_g), "conv_ln_b": row(conv_ln_b),
        "w_conv_out": w_conv_out[l].astype(BF16),
        "lq1": row(lambda_q1), "lk1": row(lambda_k1), "lq2": row(lambda_q2), "lk2": row(lambda_k2),
        "subln_g": subln_g[l].reshape(-1, 1).astype(F32), "w_attn_out": w_attn_out[l].astype(BF16), "w_o": w_o[l].astype(BF16),
        "ln2_g": row(ln2_g), "ln2_b": row(ln2_b),
        "ffn2_wg": ffn2_w_gu[l, :, :D_FF].astype(BF16), "ffn2_wu": ffn2_w_gu[l, :, D_FF:].astype(BF16),
        "ffn2_wd": ffn2_w_down[l].astype(BF16), "ln3_g": row(ln3_g), "ln3_b": row(ln3_b),
    }


def kernel(x_prompt, x_sample, rel_bias_table, ffn1_w_gu, ffn1_w_down, ln1_g, ln1_b, w_in, b_gate, conv_w_dw, conv_b_dw, conv_ln_g, conv_ln_b, w_conv_out, lambda_q1, lambda_k1, lambda_q2, lambda_k2, subln_g, w_attn_out, w_o, ln2_g, ln2_b, ffn2_w_gu, ffn2_w_down, ln3_g, ln3_b):
    p = _prepare(rel_bias_table, ffn1_w_gu, ffn1_w_down, ln1_g, ln1_b, w_in, b_gate, conv_w_dw, conv_b_dw,
                 conv_ln_g, conv_ln_b, w_conv_out, lambda_q1, lambda_k1, lambda_q2, lambda_k2, subln_g,
                 w_attn_out, w_o, ln2_g, ln2_b, ffn2_w_gu, ffn2_w_down, ln3_g, ln3_b)
    tile = min(ATTN_TILE, x_prompt.shape[1], x_sample.shape[1])
    bias = _bias_tiles(p["table"], tile)
    return (_layer(x_prompt, p, bias, tile), _layer(x_sample, p, bias, tile))
```

```python
import functools
import math

import jax
import jax.numpy as jnp
from jax import lax
from jax.experimental import pallas as pl
from jax.experimental.pallas import tpu as pltpu

F32 = jnp.float32
BF16 = jnp.bfloat16

D_MODEL = 1024
N_HEADS = 4
HEAD_DIM = 64
V_DIM = 2 * HEAD_DIM
ATTN_WIDTH = N_HEADS * 2 * HEAD_DIM
CONV_WIDTH = 512
CONV_KERNEL = 31
CONV_HALO = 16
D_FF = 2816
NUM_BUCKETS = 32
MAX_DISTANCE = 128
DEPTH = 1
ALPHA = (2.0 * DEPTH) ** 0.25
LN_EPS = 1e-5
ATTN_SCALE = HEAD_DIM ** -0.5
LAMBDA_INIT = 0.8 - 0.6 * math.exp(-0.3 * 0)
LOG2E = math.log2(math.e)
ONES_ROWS = 16
COL_U = 2 * CONV_WIDTH
COL_Q = ATTN_WIDTH
COL_K = ATTN_WIDTH
COL_V = N_HEADS * V_DIM

V7X_VMEM_LIMIT_BYTES = 56 * 1024 * 1024
FF_CHUNK = 256
TOKEN_TILE = 1024
CONV_TILE = 256
CONV_ROWS = 64
ATTN_TILE = 1024


def _dot(a, b):
    return jnp.dot(a, b, preferred_element_type=F32)


def _layer_norm(z, g, b):
    mu = jnp.mean(z, axis=-1, keepdims=True)
    zc = z - mu
    var = jnp.mean(zc * zc, axis=-1, keepdims=True)
    return zc * lax.rsqrt(var + LN_EPS) * g + b


def _resident(shape):
    return pl.BlockSpec(shape, lambda *_: (0,) * len(shape), pipeline_mode=pl.Buffered(1))


def _params(*semantics):
    return pltpu.CompilerParams(dimension_semantics=semantics,
                                vmem_limit_bytes=V7X_VMEM_LIMIT_BYTES)


def _ffn_ln_kernel(x_ref, wg_ref, wu_ref, wd_ref, g_ref, b_ref, o_ref, h_ref):
    x = x_ref[...]
    xb = x.astype(BF16)
    for c in range(D_FF // FF_CHUNK):
        sl = slice(c * FF_CHUNK, (c + 1) * FF_CHUNK)
        a = _dot(xb, wg_ref[:, sl])
        u = _dot(xb, wu_ref[:, sl])
        h_ref[:, sl] = (a * jax.nn.sigmoid(a) * u).astype(BF16)
    y = _dot(h_ref[...], wd_ref[...])
    o_ref[...] = _layer_norm(ALPHA * x + 0.5 * y, g_ref[...], b_ref[...])


def _ffn_ln(x, wg, wu, wd, g, b):
    t = x.shape[0]
    tm = min(TOKEN_TILE, t)
    row = pl.BlockSpec((tm, D_MODEL), lambda i: (i, 0))
    return pl.pallas_call(
        _ffn_ln_kernel,
        grid=(t // tm,),
        in_specs=[row, _resident((D_MODEL, D_FF)), _resident((D_MODEL, D_FF)),
                  _resident((D_FF, D_MODEL)), _resident((1, D_MODEL)), _resident((1, D_MODEL))],
        out_specs=row,
        out_shape=jax.ShapeDtypeStruct((t, D_MODEL), F32),
        scratch_shapes=[pltpu.VMEM((tm, D_FF), BF16)],
        compiler_params=_params("parallel"),
        name="ffn_ln",
    )(x, wg, wu, wd, g, b)


def _in_proj_kernel(x_ref, w_ref, hg_ref, qt_ref, k_ref, vt_ref):
    xb = x_ref[...].astype(BF16)
    u = _dot(xb, w_ref[:, 0:COL_U])
    hg_ref[...] = u[:, :CONV_WIDTH] * jax.nn.sigmoid(u[:, CONV_WIDTH:])
    o = COL_U
    qt_ref[0] = (_dot(xb, w_ref[:, o:o + COL_Q]) * (ATTN_SCALE * LOG2E)).T.astype(BF16)
    o += COL_Q
    k_ref[...] = _dot(xb, w_ref[:, o:o + COL_K]).astype(BF16)
    o += COL_K
    vt_ref[0] = _dot(xb, w_ref[:, o:o + COL_V]).T.astype(BF16)


def _in_proj(x1, w_uqkv, batch, seq):
    t = x1.shape[0]
    tm = min(TOKEN_TILE, seq)
    per_seq = seq // tm
    ncol = COL_U + COL_Q + COL_K + COL_V
    return pl.pallas_call(
        _in_proj_kernel,
        grid=(t // tm,),
        in_specs=[pl.BlockSpec((tm, D_MODEL), lambda i: (i, 0)), _resident((D_MODEL, ncol))],
        out_specs=[pl.BlockSpec((tm, CONV_WIDTH), lambda i: (i, 0)),
                   pl.BlockSpec((1, ATTN_WIDTH, tm), lambda i: (i // per_seq, 0, i % per_seq)),
                   pl.BlockSpec((tm, ATTN_WIDTH), lambda i: (i, 0)),
                   pl.BlockSpec((1, COL_V, tm), lambda i: (i // per_seq, 0, i % per_seq))],
        out_shape=[jax.ShapeDtypeStruct((t, CONV_WIDTH), F32),
                   jax.ShapeDtypeStruct((batch, ATTN_WIDTH, seq), BF16),
                   jax.ShapeDtypeStruct((t, ATTN_WIDTH), BF16),
                   jax.ShapeDtypeStruct((batch, COL_V, seq), BF16)],
        compiler_params=_params("parallel"),
        name="in_proj",
    )(x1, w_uqkv)


def _conv_kernel(prev_ref, cur_ref, next_ref, w_ref, b_ref, g_ref, beta_ref, o_ref, ext_ref, *, tc):
    i = pl.program_id(1)
    last = pl.num_programs(1) - 1
    ext_ref[0:CONV_HALO, :] = jnp.where(i > 0, prev_ref[0], 0.0)
    ext_ref[CONV_HALO:CONV_HALO + tc, :] = cur_ref[0]
    ext_ref[CONV_HALO + tc:2 * CONV_HALO + tc, :] = jnp.where(i < last, next_ref[0], 0.0)
    first_tap = CONV_HALO - CONV_KERNEL // 2
    for r in range(tc // CONV_ROWS):
        base = r * CONV_ROWS + first_tap
        acc = jnp.broadcast_to(b_ref[...], (CONV_ROWS, CONV_WIDTH))
        for j in range(CONV_KERNEL):
            acc = acc + w_ref[j:j + 1, :] * ext_ref[base + j:base + j + CONV_ROWS, :]
        y = _layer_norm(acc, g_ref[...], beta_ref[...])
        o_ref[0, r * CONV_ROWS:(r + 1) * CONV_ROWS, :] = (y * jax.nn.sigmoid(y)).astype(BF16)


def _conv_branch(hg, w_dw, b_dw, ln_g, ln_b, batch, seq):
    tc = min(CONV_TILE, seq)
    hpb = tc // CONV_HALO
    nhalo = seq // CONV_HALO
    hg3 = hg.reshape(batch, seq, CONV_WIDTH)
    out = pl.pallas_call(
        functools.partial(_conv_kernel, tc=tc),
        grid=(batch, seq // tc),
        in_specs=[
            pl.BlockSpec((1, CONV_HALO, CONV_WIDTH), lambda b, i: (b, jnp.maximum(i * hpb - 1, 0), 0)),
            pl.BlockSpec((1, tc, CONV_WIDTH), lambda b, i: (b, i, 0)),
            pl.BlockSpec((1, CONV_HALO, CONV_WIDTH),
                         lambda b, i: (b, jnp.minimum((i + 1) * hpb, nhalo - 1), 0)),
            _resident((CONV_KERNEL, CONV_WIDTH)), _resident((1, CONV_WIDTH)),
            _resident((1, CONV_WIDTH)), _resident((1, CONV_WIDTH))],
        out_specs=pl.BlockSpec((1, tc, CONV_WIDTH), lambda b, i: (b, i, 0)),
        out_shape=jax.ShapeDtypeStruct((batch, seq, CONV_WIDTH), BF16),
        scratch_shapes=[pltpu.VMEM((tc + 2 * CONV_HALO, CONV_WIDTH), F32)],
        compiler_params=_params("parallel", "parallel"),
        name="conv_branch",
    )(hg3, hg3, hg3, w_dw, b_dw, ln_g, ln_b)
    return out.reshape(batch * seq, CONV_WIDTH)


def _t5_bucket(rel):
    nb = NUM_BUCKETS // 2
    ret = jnp.where(rel > 0, nb, 0)
    n = jnp.abs(rel)
    max_exact = nb // 2
    nf = jnp.maximum(n, 1).astype(F32)
    large = max_exact + (jnp.log(nf / max_exact) / math.log(MAX_DISTANCE / max_exact)
                         * (nb - max_exact)).astype(jnp.int32)
    large = jnp.minimum(large, nb - 1)
    return ret + jnp.where(n < max_exact, n, large)


def _bias_kernel(tab_ref, o_ref, *, tile):
    h = pl.program_id(0)
    d = pl.program_id(1)
    key = lax.broadcasted_iota(jnp.int32, (tile, tile), 0)
    query = lax.broadcasted_iota(jnp.int32, (tile, tile), 1)
    bucket = _t5_bucket(key - query + (d - 1) * tile)
    acc = jnp.zeros((tile, tile), F32)
    for b in range(NUM_BUCKETS):
        acc = jnp.where(bucket == b, tab_ref[b, h], acc)
    o_ref[0, 0] = acc * LOG2E


def _bias_tiles(table, tile):
    return pl.pallas_call(
        functools.partial(_bias_kernel, tile=tile),
        grid=(N_HEADS, 3),
        in_specs=[pl.BlockSpec(memory_space=pltpu.SMEM)],
        out_specs=pl.BlockSpec((1, 1, tile, tile), lambda h, d: (h, d, 0, 0)),
        out_shape=jax.ShapeDtypeStruct((N_HEADS, 3, tile, tile), F32),
        compiler_params=_params("parallel", "parallel"),
        name="bias_tiles",
    )(table)


def _attn_kernel(tab_ref, qt_ref, k_ref, vt_ref, bias_ref, lq1_ref, lk1_ref, lq2_ref, lk2_ref, g_ref,
                 o_ref, acc_ref, m_ref, s_ref, mx_ref, *, tile, nk):
    h = pl.program_id(1)
    qi = pl.program_id(2)
    qt = qt_ref[0]
    row = lax.broadcasted_iota(jnp.int32, qt.shape, 0)
    zero = jnp.zeros_like(qt)
    qt_maps = (jnp.where(row < HEAD_DIM, qt, zero), jnp.where(row >= HEAD_DIM, qt, zero))
    ones = jnp.ones((ONES_ROWS, tile), BF16)
    m_ref[...] = jnp.full(m_ref.shape, -jnp.inf, F32)
    acc_ref[...] = jnp.zeros(acc_ref.shape, F32)

    def scores(ki, mi, bias):
        k0 = pl.multiple_of(ki * tile, tile)
        s = _dot(k_ref[pl.ds(k0, tile), :], qt_maps[mi])
        if bias is not None:
            s = s + bias
        s_ref[mi] = s
        mx_ref[mi] = jnp.max(s, axis=0, keepdims=True)

    def absorb(ki, mi, shift):
        k0 = pl.multiple_of(ki * tile, tile)
        m_old = m_ref[mi]
        m_new = jnp.maximum(m_old, mx_ref[mi] + shift)
        p = jnp.exp2(s_ref[mi] - (m_new - shift)).astype(BF16)
        vt_aug = jnp.concatenate([vt_ref[0, :, pl.ds(k0, tile)], ones], axis=0)
        acc_ref[mi] = jnp.exp2(m_old - m_new) * acc_ref[mi] + _dot(vt_aug, p)
        m_ref[mi] = m_new

    lo = jnp.maximum(qi - 1, 0)
    hi = jnp.minimum(qi + 2, nk)
    n_near = hi - lo
    n_far = nk - n_near
    near_bias = lambda ki: bias_ref[0, ki - qi + 1]
    far_tile = lambda j: jnp.where(j < lo, j, j + n_near)
    far_shift = lambda ki: LOG2E * jnp.where(ki < qi, tab_ref[NUM_BUCKETS // 2 - 1, h],
                                             tab_ref[NUM_BUCKETS - 1, h])

    def block(cur, cur_bias, cur_shift, nxt, nxt_bias):
        scores(cur, 1, cur_bias)
        absorb(cur, 0, cur_shift)
        if nxt is not None:
            scores(nxt, 0, nxt_bias)
        absorb(cur, 1, cur_shift)

    scores(lo, 0, near_bias(lo))

    def near(ki, carry):
        block(ki, near_bias(ki), 0.0, ki + 1, near_bias(ki + 1))
        return carry

    lax.fori_loop(lo, hi - 1, near, 0)
    block(hi - 1, near_bias(hi - 1), 0.0, far_tile(0), None)

    def far(j, carry):
        ki = far_tile(j)
        block(ki, None, far_shift(ki), far_tile(j + 1), None)
        return carry

    lax.fori_loop(0, n_far - 1, far, 0)
    last = far_tile(n_far - 1)
    block(last, None, far_shift(last), None, None)

    lam = (jnp.exp(jnp.sum(lq1_ref[...] * lk1_ref[...], keepdims=True))
           - jnp.exp(jnp.sum(lq2_ref[...] * lk2_ref[...], keepdims=True)) + LAMBDA_INIT)
    o1 = acc_ref[0]
    o2 = acc_ref[1]
    a = o1[:V_DIM] / o1[V_DIM:V_DIM + 1] - lam * (o2[:V_DIM] / o2[V_DIM:V_DIM + 1])
    r = a * lax.rsqrt(jnp.mean(a * a, axis=0, keepdims=True) + LN_EPS) * g_ref[...]
    o_ref[...] = (r * (1.0 - LAMBDA_INIT)).T.astype(BF16)


def _attention(qt, k, vt, table, bias, lq1, lk1, lq2, lk2, subln_g, batch, seq, tile):
    nk = seq // tile
    assert nk >= 4, "the far-tile pipeline needs at least one tile beyond the three near ones"
    small = _resident((1, HEAD_DIM))
    return pl.pallas_call(
        functools.partial(_attn_kernel, tile=tile, nk=nk),
        grid=(batch, N_HEADS, nk),
        in_specs=[
            pl.BlockSpec(memory_space=pltpu.SMEM),
            pl.BlockSpec((1, V_DIM, tile), lambda b, h, i: (b, h, i)),
            pl.BlockSpec((seq, V_DIM), lambda b, h, i: (b, h)),
            pl.BlockSpec((1, V_DIM, seq), lambda b, h, i: (b, h, 0)),
            pl.BlockSpec((1, 3, tile, tile), lambda b, h, i: (h, 0, 0, 0), pipeline_mode=pl.Buffered(1)),
            small, small, small, small, _resident((V_DIM, 1))],
        out_specs=pl.BlockSpec((tile, V_DIM), lambda b, h, i: (b * nk + i, h)),
        out_shape=jax.ShapeDtypeStruct((batch * seq, ATTN_WIDTH), BF16),
        scratch_shapes=[pltpu.VMEM((2, V_DIM + ONES_ROWS, tile), F32), pltpu.VMEM((2, 1, tile), F32),
                        pltpu.VMEM((2, tile, tile), F32), pltpu.VMEM((2, 1, tile), F32)],
        compiler_params=_params("parallel", "parallel", "parallel"),
        name="diff_attention",
    )(table, qt, k, vt, bias, lq1, lk1, lq2, lk2, subln_g)


def _merge_ln_kernel(x_ref, hc_ref, att_ref, wco_ref, wao_ref, wg_ref, bg_ref, wo_ref, g_ref, b_ref, o_ref):
    x = x_ref[...]
    xb = x.astype(BF16)
    y_conv = _dot(hc_ref[...], wco_ref[...])
    y_attn = _dot(att_ref[...], wao_ref[...])
    g_conv = jax.nn.sigmoid(_dot(xb, wg_ref[:, :D_MODEL]) + bg_ref[:, :D_MODEL])
    g_attn = jax.nn.sigmoid(_dot(xb, wg_ref[:, D_MODEL:]) + bg_ref[:, D_MODEL:])
    merged = g_conv * y_conv + g_attn * y_attn
    m = _dot(merged.astype(BF16), wo_ref[...])
    o_ref[...] = _layer_norm(ALPHA * x + m, g_ref[...], b_ref[...])


def _merge_ln(x1, hc, att, wco, wao, wg, bg, wo, g, b):
    t = x1.shape[0]
    tm = min(TOKEN_TILE, t)
    row = pl.BlockSpec((tm, D_MODEL), lambda i: (i, 0))
    half = pl.BlockSpec((tm, CONV_WIDTH), lambda i: (i, 0))
    return pl.pallas_call(
        _merge_ln_kernel,
        grid=(t // tm,),
        in_specs=[row, half, half, _resident((CONV_WIDTH, D_MODEL)), _resident((ATTN_WIDTH, D_MODEL)),
                  _resident((D_MODEL, 2 * D_MODEL)), _resident((1, 2 * D_MODEL)),
                  _resident((D_MODEL, D_MODEL)), _resident((1, D_MODEL)), _resident((1, D_MODEL))],
        out_specs=row,
        out_shape=jax.ShapeDtypeStruct((t, D_MODEL), F32),
        compiler_params=_params("parallel"),
        name="merge_ln",
    )(x1, hc, att, wco, wao, wg, bg, wo, g, b)


def _layer(x, p, bias, tile):
    batch, seq, _ = x.shape
    x0 = x.reshape(batch * seq, D_MODEL)
    x1 = _ffn_ln(x0, p["ffn1_wg"], p["ffn1_wu"], p["ffn1_wd"], p["ln1_g"], p["ln1_b"])
    hg, qt, k, vt = _in_proj(x1, p["w_uqkv"], batch, seq)
    hc = _conv_branch(hg, p["conv_w"], p["conv_b"], p["conv_ln_g"], p["conv_ln_b"], batch, seq)
    att = _attention(qt, k, vt, p["table"], bias, p["lq1"], p["lk1"], p["lq2"], p["lk2"], p["subln_g"],
                     batch, seq, tile)
    x2 = _merge_ln(x1, hc, att, p["w_conv_out"], p["w_attn_out"], p["w_gate"], p["b_gate"], p["w_o"],
                   p["ln2_g"], p["ln2_b"])
    x3 = _ffn_ln(x2, p["ffn2_wg"], p["ffn2_wu"], p["ffn2_wd"], p["ln3_g"], p["ln3_b"])
    return x3.reshape(batch, seq, D_MODEL)


def _prepare(rel_bias_table, ffn1_w_gu, ffn1_w_down, ln1_g, ln1_b, w_in, b_gate, conv_w_dw, conv_b_dw,
             conv_ln_g, conv_ln_b, w_conv_out, lambda_q1, lambda_k1, lambda_q2, lambda_k2, subln_g,
             w_attn_out, w_o, ln2_g, ln2_b, ffn2_w_gu, ffn2_w_down, ln3_g, ln3_b):
    l = 0
    n_uqkv = COL_U + COL_Q + COL_K + COL_V
    row = lambda a: a[l].reshape(1, -1).astype(F32)
    return {
        "table": rel_bias_table.astype(F32),
        "ffn1_wg": ffn1_w_gu[l, :, :D_FF].astype(BF16), "ffn1_wu": ffn1_w_gu[l, :, D_FF:].astype(BF16),
        "ffn1_wd": ffn1_w_down[l].astype(BF16), "ln1_g": row(ln1_g), "ln1_b": row(ln1_b),
        "w_uqkv": w_in[l, :, :n_uqkv].astype(BF16), "w_gate": w_in[l, :, n_uqkv:].astype(BF16),
        "b_gate": row(b_gate),
        "conv_w": conv_w_dw[l].reshape(CONV_KERNEL, CONV_WIDTH).astype(F32), "conv_b": row(conv_b_dw),
        "conv_ln_g": row(conv_ln_g), "conv_ln_b": row(conv_ln_b),
        "w_conv_out": w_conv_out[l].astype(BF16),
        "lq1": row(lambda_q1), "lk1": row(lambda_k1), "lq2": row(lambda_q2), "lk2": row(lambda_k2),
        "subln_g": subln_g[l].reshape(-1, 1).astype(F32), "w_attn_out": w_attn_out[l].astype(BF16), "w_o": w_o[l].astype(BF16),
        "ln2_g": row(ln2_g), "ln2_b": row(ln2_b),
        "ffn2_wg": ffn2_w_gu[l, :, :D_FF].astype(BF16), "ffn2_wu": ffn2_w_gu[l, :, D_FF:].astype(BF16),
        "ffn2_wd": ffn2_w_down[l].astype(BF16), "ln3_g": row(ln3_g), "ln3_b": row(ln3_b),
    }


def kernel(x_prompt, x_sample, rel_bias_table, ffn1_w_gu, ffn1_w_down, ln1_g, ln1_b, w_in, b_gate, conv_w_dw, conv_b_dw, conv_ln_g, conv_ln_b, w_conv_out, lambda_q1, lambda_k1, lambda_q2, lambda_k2, subln_g, w_attn_out, w_o, ln2_g, ln2_b, ffn2_w_gu, ffn2_w_down, ln3_g, ln3_b):
    p = _prepare(rel_bias_table, ffn1_w_gu, ffn1_w_down, ln1_g, ln1_b, w_in, b_gate, conv_w_dw, conv_b_dw,
                 conv_ln_g, conv_ln_b, w_conv_out, lambda_q1, lambda_k1, lambda_q2, lambda_k2, subln_g,
                 w_attn_out, w_o, ln2_g, ln2_b, ffn2_w_gu, ffn2_w_down, ln3_g, ln3_b)
    tile = min(ATTN_TILE, x_prompt.shape[1], x_sample.shape[1])
    bias = _bias_tiles(p["table"], tile)
    return (_layer(x_prompt, p, bias, tile), _layer(x_sample, p, bias, tile))
```

```python
import functools
import math

import jax
import jax.numpy as jnp
from jax import lax
from jax.experimental import pallas as pl
from jax.experimental.pallas import tpu as pltpu

F32 = jnp.float32
BF16 = jnp.bfloat16

D_MODEL = 1024
N_HEADS = 4
HEAD_DIM = 64
V_DIM = 2 * HEAD_DIM
ATTN_WIDTH = N_HEADS * 2 * HEAD_DIM
CONV_WIDTH = 512
CONV_KERNEL = 31
CONV_HALO = 16
D_FF = 2816
NUM_BUCKETS = 32
MAX_DISTANCE = 128
DEPTH = 1
ALPHA = (2.0 * DEPTH) ** 0.25
LN_EPS = 1e-5
ATTN_SCALE = HEAD_DIM ** -0.5
LAMBDA_INIT = 0.8 - 0.6 * math.exp(-0.3 * 0)
LOG2E = math.log2(math.e)
ONES_ROWS = 16
COL_U = 2 * CONV_WIDTH
COL_Q = ATTN_WIDTH
COL_K = ATTN_WIDTH
COL_V = N_HEADS * V_DIM

V7X_VMEM_LIMIT_BYTES = 56 * 1024 * 1024
FF_CHUNK = 256
TOKEN_TILE = 1024
CONV_TILE = 256
CONV_ROWS = 64
CONV_LANES = 256
SUBLANES = 8
ATTN_TILE = 1024
NEAR_TILES = 3
FAR_UNROLL = 4


def _dot(a, b):
    return jnp.dot(a, b, preferred_element_type=F32)


def _layer_norm(z, g, b):
    mu = jnp.mean(z, axis=-1, keepdims=True)
    zc = z - mu
    var = jnp.mean(zc * zc, axis=-1, keepdims=True)
    return zc * lax.rsqrt(var + LN_EPS) * g + b


def _resident(shape):
    return pl.BlockSpec(shape, lambda *_: (0,) * len(shape), pipeline_mode=pl.Buffered(1))


def _params(*semantics):
    return pltpu.CompilerParams(dimension_semantics=semantics,
                                vmem_limit_bytes=V7X_VMEM_LIMIT_BYTES)


def _ffn_ln_kernel(x_ref, wg_ref, wu_ref, wd_ref, g_ref, b_ref, o_ref, h_ref):
    x = x_ref[...]
    xb = x.astype(BF16)
    for c in range(D_FF // FF_CHUNK):
        sl = slice(c * FF_CHUNK, (c + 1) * FF_CHUNK)
        a = _dot(xb, wg_ref[:, sl])
        u = _dot(xb, wu_ref[:, sl])
        h_ref[:, sl] = (a * jax.nn.sigmoid(a) * u).astype(BF16)
    y = _dot(h_ref[...], wd_ref[...])
    o_ref[...] = _layer_norm(ALPHA * x + 0.5 * y, g_ref[...], b_ref[...])


def _ffn_ln(x, wg, wu, wd, g, b):
    t = x.shape[0]
    tm = min(TOKEN_TILE, t)
    row = pl.BlockSpec((tm, D_MODEL), lambda i: (i, 0))
    return pl.pallas_call(
        _ffn_ln_kernel,
        grid=(t // tm,),
        in_specs=[row, _resident((D_MODEL, D_FF)), _resident((D_MODEL, D_FF)),
                  _resident((D_FF, D_MODEL)), _resident((1, D_MODEL)), _resident((1, D_MODEL))],
        out_specs=row,
        out_shape=jax.ShapeDtypeStruct((t, D_MODEL), F32),
        scratch_shapes=[pltpu.VMEM((tm, D_FF), BF16)],
        compiler_params=_params("parallel"),
        name="ffn_ln",
    )(x, wg, wu, wd, g, b)


def _in_proj_kernel(x_ref, w_ref, hg_ref, qt_ref, k_ref, vt_ref):
    xb = x_ref[...].astype(BF16)
    u = _dot(xb, w_ref[:, 0:COL_U])
    hg_ref[...] = u[:, :CONV_WIDTH] * jax.nn.sigmoid(u[:, CONV_WIDTH:])
    o = COL_U
    qt_ref[0] = (_dot(xb, w_ref[:, o:o + COL_Q]) * (ATTN_SCALE * LOG2E)).T.astype(BF16)
    o += COL_Q
    k_ref[...] = _dot(xb, w_ref[:, o:o + COL_K]).astype(BF16)
    o += COL_K
    vt_ref[0] = _dot(xb, w_ref[:, o:o + COL_V]).T.astype(BF16)


def _in_proj(x1, w_uqkv, batch, seq):
    t = x1.shape[0]
    tm = min(TOKEN_TILE, seq)
    per_seq = seq // tm
    ncol = COL_U + COL_Q + COL_K + COL_V
    return pl.pallas_call(
        _in_proj_kernel,
        grid=(t // tm,),
        in_specs=[pl.BlockSpec((tm, D_MODEL), lambda i: (i, 0)), _resident((D_MODEL, ncol))],
        out_specs=[pl.BlockSpec((tm, CONV_WIDTH), lambda i: (i, 0)),
                   pl.BlockSpec((1, ATTN_WIDTH, tm), lambda i: (i // per_seq, 0, i % per_seq)),
                   pl.BlockSpec((tm, ATTN_WIDTH), lambda i: (i, 0)),
                   pl.BlockSpec((1, COL_V, tm), lambda i: (i // per_seq, 0, i % per_seq))],
        out_shape=[jax.ShapeDtypeStruct((t, CONV_WIDTH), F32),
                   jax.ShapeDtypeStruct((batch, ATTN_WIDTH, seq), BF16),
                   jax.ShapeDtypeStruct((t, ATTN_WIDTH), BF16),
                   jax.ShapeDtypeStruct((batch, COL_V, seq), BF16)],
        compiler_params=_params("parallel"),
        name="in_proj",
    )(x1, w_uqkv)


def _conv_kernel(prev_ref, cur_ref, next_ref, w_ref, b_ref, g_ref, beta_ref, o_ref, ext_ref, *, tc):
    i = pl.program_id(1)
    last = pl.num_programs(1) - 1
    ext_ref[0:CONV_HALO, :] = jnp.where(i > 0, prev_ref[0], 0.0)
    ext_ref[CONV_HALO:CONV_HALO + tc, :] = cur_ref[0]
    ext_ref[CONV_HALO + tc:2 * CONV_HALO + tc, :] = jnp.where(i < last, next_ref[0], 0.0)
    first_tap = CONV_HALO - CONV_KERNEL // 2
    for r in range(tc // CONV_ROWS):
        r0 = r * CONV_ROWS
        halves = []
        for c in range(CONV_WIDTH // CONV_LANES):
            cs = slice(c * CONV_LANES, (c + 1) * CONV_LANES)
            acc = None
            for b in range(SUBLANES):
                u = None
                for j in range(CONV_KERNEL):
                    if (first_tap + j) % SUBLANES != b:
                        continue
                    e0 = r0 + first_tap + j - b
                    term = w_ref[j:j + 1, cs] * ext_ref[e0:e0 + CONV_ROWS + SUBLANES, cs]
                    u = term if u is None else u + term
                u = u[b:b + CONV_ROWS]
                acc = u if acc is None else acc + u
            halves.append(acc)
        y = _layer_norm(jnp.concatenate(halves, axis=1) + b_ref[...], g_ref[...], beta_ref[...])
        o_ref[0, r0:r0 + CONV_ROWS, :] = (y * jax.nn.sigmoid(y)).astype(BF16)


def _conv_branch(hg, w_dw, b_dw, ln_g, ln_b, batch, seq):
    tc = min(CONV_TILE, seq)
    hpb = tc // CONV_HALO
    nhalo = seq // CONV_HALO
    hg3 = hg.reshape(batch, seq, CONV_WIDTH)
    out = pl.pallas_call(
        functools.partial(_conv_kernel, tc=tc),
        grid=(batch, seq // tc),
        in_specs=[
            pl.BlockSpec((1, CONV_HALO, CONV_WIDTH), lambda b, i: (b, jnp.maximum(i * hpb - 1, 0), 0)),
            pl.BlockSpec((1, tc, CONV_WIDTH), lambda b, i: (b, i, 0)),
            pl.BlockSpec((1, CONV_HALO, CONV_WIDTH),
                         lambda b, i: (b, jnp.minimum((i + 1) * hpb, nhalo - 1), 0)),
            _resident((CONV_KERNEL, CONV_WIDTH)), _resident((1, CONV_WIDTH)),
            _resident((1, CONV_WIDTH)), _resident((1, CONV_WIDTH))],
        out_specs=pl.BlockSpec((1, tc, CONV_WIDTH), lambda b, i: (b, i, 0)),
        out_shape=jax.ShapeDtypeStruct((batch, seq, CONV_WIDTH), BF16),
        scratch_shapes=[pltpu.VMEM((tc + 2 * CONV_HALO, CONV_WIDTH), F32)],
        compiler_params=_params("parallel", "parallel"),
        name="conv_branch",
    )(hg3, hg3, hg3, w_dw, b_dw, ln_g, ln_b)
    return out.reshape(batch * seq, CONV_WIDTH)


def _t5_bucket(rel):
    nb = NUM_BUCKETS // 2
    ret = jnp.where(rel > 0, nb, 0)
    n = jnp.abs(rel)
    max_exact = nb // 2
    nf = jnp.maximum(n, 1).astype(F32)
    large = max_exact + (jnp.log(nf / max_exact) / math.log(MAX_DISTANCE / max_exact)
                         * (nb - max_exact)).astype(jnp.int32)
    large = jnp.minimum(large, nb - 1)
    return ret + jnp.where(n < max_exact, n, large)


def _bias_kernel(tab_ref, o_ref, *, tile):
    h = pl.program_id(0)
    d = pl.program_id(1)
    key = lax.broadcasted_iota(jnp.int32, (tile, tile), 0)
    query = lax.broadcasted_iota(jnp.int32, (tile, tile), 1)
    bucket = _t5_bucket(key - query + (d - 1) * tile)
    acc = jnp.zeros((tile, tile), F32)
    for b in range(NUM_BUCKETS):
        acc = jnp.where(bucket == b, tab_ref[b, h], acc)
    o_ref[0, 0] = acc * LOG2E


def _bias_tiles(table, tile):
    return pl.pallas_call(
        functools.partial(_bias_kernel, tile=tile),
        grid=(N_HEADS, 3),
        in_specs=[pl.BlockSpec(memory_space=pltpu.SMEM)],
        out_specs=pl.BlockSpec((1, 1, tile, tile), lambda h, d: (h, d, 0, 0)),
        out_shape=jax.ShapeDtypeStruct((N_HEADS, 3, tile, tile), F32),
        compiler_params=_params("parallel", "parallel"),
        name="bias_tiles",
    )(table)


def _attn_kernel(tab_ref, qt_ref, k_ref, vt_ref, bias_ref, lq1_ref, lk1_ref, lq2_ref, lk2_ref, g_ref,
                 o_ref, acc_ref, m_ref, s_ref, mx_ref, *, tile, nk):
    h = pl.program_id(1)
    qi = pl.program_id(2)
    qt = qt_ref[0]
    row = lax.broadcasted_iota(jnp.int32, qt.shape, 0)
    zero = jnp.zeros_like(qt)
    qt_maps = (jnp.where(row < HEAD_DIM, qt, zero), jnp.where(row >= HEAD_DIM, qt, zero))
    ones = jnp.ones((ONES_ROWS, tile), BF16)
    m_ref[...] = jnp.full(m_ref.shape, -jnp.inf, F32)
    acc_ref[...] = jnp.zeros(acc_ref.shape, F32)

    def scores(ki, mi, bias):
        k0 = pl.multiple_of(ki * tile, tile)
        s = _dot(k_ref[pl.ds(k0, tile), :], qt_maps[mi])
        if bias is not None:
            s = s + bias
        s_ref[mi] = s
        mx_ref[mi] = jnp.max(s, axis=0, keepdims=True)

    def absorb(ki, mi, shift):
        k0 = pl.multiple_of(ki * tile, tile)
        m_old = m_ref[mi]
        m_new = jnp.maximum(m_old, mx_ref[mi] + shift)
        p = jnp.exp2(s_ref[mi] - (m_new - shift)).astype(BF16)
        vt_aug = jnp.concatenate([vt_ref[0, :, pl.ds(k0, tile)], ones], axis=0)
        acc_ref[mi] = jnp.exp2(m_old - m_new) * acc_ref[mi] + _dot(vt_aug, p)
        m_ref[mi] = m_new

    far_shift = lambda ki: LOG2E * jnp.where(ki < qi, tab_ref[NUM_BUCKETS // 2 - 1, h],
                                             tab_ref[NUM_BUCKETS - 1, h])

    w0 = jnp.clip(qi - 1, 0, nk - NEAR_TILES)

    def near_stage(pos):
        ki = w0 + pos
        d = ki - qi
        banded = jnp.abs(d) <= 1
        bias = jnp.where(banded, bias_ref[0, jnp.clip(d, -1, 1) + 1], 0.0)
        return ki, bias, jnp.where(banded, 0.0, far_shift(ki))

    def far_stage(j):
        ki = jnp.where(j < w0, j, j + NEAR_TILES)
        return ki, None, far_shift(ki)

    def block(cur, nxt):
        ki, bias, shift = cur
        scores(ki, 1, bias)
        absorb(ki, 0, shift)
        if nxt is not None:
            scores(nxt[0], 0, nxt[1])
        absorb(ki, 1, shift)

    def run(stages, after):
        for cur, nxt in zip(stages, stages[1:] + [after]):
            block(cur, nxt)

    n_far = nk - NEAR_TILES
    n_groups, n_lead = divmod(n_far, FAR_UNROLL)
    lead = [near_stage(pos) for pos in range(NEAR_TILES)] + [far_stage(j) for j in range(n_lead)]
    group = lambda g: [far_stage(n_lead + g * FAR_UNROLL + u) for u in range(FAR_UNROLL)]
    scores(lead[0][0], 0, lead[0][1])
    run(lead, group(0)[0] if n_groups else None)
    if n_groups:
        def far(g, carry):
            run(group(g), group(g + 1)[0])
            return carry

        def far_last(_, carry):
            run(group(n_groups - 1), None)
            return carry

        lax.fori_loop(0, n_groups - 1, far, 0)
        lax.fori_loop(0, jnp.minimum(qi, 0) + 1, far_last, 0)

    lam = (jnp.exp(jnp.sum(lq1_ref[...] * lk1_ref[...], keepdims=True))
           - jnp.exp(jnp.sum(lq2_ref[...] * lk2_ref[...], keepdims=True)) + LAMBDA_INIT)
    o1 = acc_ref[0]
    o2 = acc_ref[1]
    a = o1[:V_DIM] / o1[V_DIM:V_DIM + 1] - lam * (o2[:V_DIM] / o2[V_DIM:V_DIM + 1])
    r = a * lax.rsqrt(jnp.mean(a * a, axis=0, keepdims=True) + LN_EPS) * g_ref[...]
    o_ref[...] = (r * (1.0 - LAMBDA_INIT)).T.astype(BF16)


def _attention(qt, k, vt, table, bias, lq1, lk1, lq2, lk2, subln_g, batch, seq, tile):
    nk = seq // tile
    assert nk >= NEAR_TILES and tile >= MAX_DISTANCE
    small = _resident((1, HEAD_DIM))
    return pl.pallas_call(
        functools.partial(_attn_kernel, tile=tile, nk=nk),
        grid=(batch, N_HEADS, nk),
        in_specs=[
            pl.BlockSpec(memory_space=pltpu.SMEM),
            pl.BlockSpec((1, V_DIM, tile), lambda b, h, i: (b, h, i)),
            pl.BlockSpec((seq, V_DIM), lambda b, h, i: (b, h)),
            pl.BlockSpec((1, V_DIM, seq), lambda b, h, i: (b, h, 0)),
            pl.BlockSpec((1, 3, tile, tile), lambda b, h, i: (h, 0, 0, 0), pipeline_mode=pl.Buffered(1)),
            small, small, small, small, _resident((V_DIM, 1))],
        out_specs=pl.BlockSpec((tile, V_DIM), lambda b, h, i: (b * nk + i, h)),
        out_shape=jax.ShapeDtypeStruct((batch * seq, ATTN_WIDTH), BF16),
        scratch_shapes=[pltpu.VMEM((2, V_DIM + ONES_ROWS, tile), F32), pltpu.VMEM((2, 1, tile), F32),
                        pltpu.VMEM((2, tile, tile), F32), pltpu.VMEM((2, 1, tile), F32)],
        compiler_params=_params("parallel", "parallel", "parallel"),
        name="diff_attention",
    )(table, qt, k, vt, bias, lq1, lk1, lq2, lk2, subln_g)


def _merge_ln_kernel(x_ref, hc_ref, att_ref, wco_ref, wao_ref, wg_ref, bg_ref, wo_ref, g_ref, b_ref, o_ref):
    x = x_ref[...]
    xb = x.astype(BF16)
    y_conv = _dot(hc_ref[...], wco_ref[...])
    y_attn = _dot(att_ref[...], wao_ref[...])
    g_conv = jax.nn.sigmoid(_dot(xb, wg_ref[:, :D_MODEL]) + bg_ref[:, :D_MODEL])
    g_attn = jax.nn.sigmoid(_dot(xb, wg_ref[:, D_MODEL:]) + bg_ref[:, D_MODEL:])
    merged = g_conv * y_conv + g_attn * y_attn
    m = _dot(merged.astype(BF16), wo_ref[...])
    o_ref[...] = _layer_norm(ALPHA * x + m, g_ref[...], b_ref[...])


def _merge_ln(x1, hc, att, wco, wao, wg, bg, wo, g, b):
    t = x1.shape[0]
    tm = min(TOKEN_TILE, t)
    row = pl.BlockSpec((tm, D_MODEL), lambda i: (i, 0))
    half = pl.BlockSpec((tm, CONV_WIDTH), lambda i: (i, 0))
    return pl.pallas_call(
        _merge_ln_kernel,
        grid=(t // tm,),
        in_specs=[row, half, half, _resident((CONV_WIDTH, D_MODEL)), _resident((ATTN_WIDTH, D_MODEL)),
                  _resident((D_MODEL, 2 * D_MODEL)), _resident((1, 2 * D_MODEL)),
                  _resident((D_MODEL, D_MODEL)), _resident((1, D_MODEL)), _resident((1, D_MODEL))],
        out_specs=row,
        out_shape=jax.ShapeDtypeStruct((t, D_MODEL), F32),
        compiler_params=_params("parallel"),
        name="merge_ln",
    )(x1, hc, att, wco, wao, wg, bg, wo, g, b)


def _layer(x, p, bias, tile):
    batch, seq, _ = x.shape
    x0 = x.reshape(batch * seq, D_MODEL)
    x1 = _ffn_ln(x0, p["ffn1_wg"], p["ffn1_wu"], p["ffn1_wd"], p["ln1_g"], p["ln1_b"])
    hg, qt, k, vt = _in_proj(x1, p["w_uqkv"], batch, seq)
    hc = _conv_branch(hg, p["conv_w"], p["conv_b"], p["conv_ln_g"], p["conv_ln_b"], batch, seq)
    att = _attention(qt, k, vt, p["table"], bias, p["lq1"], p["lk1"], p["lq2"], p["lk2"], p["subln_g"],
                     batch, seq, tile)
    x2 = _merge_ln(x1, hc, att, p["w_conv_out"], p["w_attn_out"], p["w_gate"], p["b_gate"], p["w_o"],
                   p["ln2_g"], p["ln2_b"])
    x3 = _ffn_ln(x2, p["ffn2_wg"], p["ffn2_wu"], p["ffn2_wd"], p["ln3_g"], p["ln3_b"])
    return x3.reshape(batch, seq, D_MODEL)


def _prepare(rel_bias_table, ffn1_w_gu, ffn1_w_down, ln1_g, ln1_b, w_in, b_gate, conv_w_dw, conv_b_dw,
             conv_ln_g, conv_ln_b, w_conv_out, lambda_q1, lambda_k1, lambda_q2, lambda_k2, subln_g,
             w_attn_out, w_o, ln2_g, ln2_b, ffn2_w_gu, ffn2_w_down, ln3_g, ln3_b):
    l = 0
    n_uqkv = COL_U + COL_Q + COL_K + COL_V
    row = lambda a: a[l].reshape(1, -1).astype(F32)
    return {
        "table": rel_bias_table.astype(F32),
        "ffn1_wg": ffn1_w_gu[l, :, :D_FF].astype(BF16), "ffn1_wu": ffn1_w_gu[l, :, D_FF:].astype(BF16),
        "ffn1_wd": ffn1_w_down[l].astype(BF16), "ln1_g": row(ln1_g), "ln1_b": row(ln1_b),
        "w_uqkv": w_in[l, :, :n_uqkv].astype(BF16), "w_gate": w_in[l, :, n_uqkv:].astype(BF16),
        "b_gate": row(b_gate),
        "conv_w": conv_w_dw[l].reshape(CONV_KERNEL, CONV_WIDTH).astype(F32), "conv_b": row(conv_b_dw),
        "conv_ln_g": row(conv_ln_g), "conv_ln_b": row(conv_ln_b),
        "w_conv_out": w_conv_out[l].astype(BF16),
        "lq1": row(lambda_q1), "lk1": row(lambda_k1), "lq2": row(lambda_q2), "lk2": row(lambda_k2),
        "subln_g": subln_g[l].reshape(-1, 1).astype(F32), "w_attn_out": w_attn_out[l].astype(BF16), "w_o": w_o[l].astype(BF16),
        "ln2_g": row(ln2_g), "ln2_b": row(ln2_b),
        "ffn2_wg": ffn2_w_gu[l, :, :D_FF].astype(BF16), "ffn2_wu": ffn2_w_gu[l, :, D_FF:].astype(BF16),
        "ffn2_wd": ffn2_w_down[l].astype(BF16), "ln3_g": row(ln3_g), "ln3_b": row(ln3_b),
    }


def kernel(x_prompt, x_sample, rel_bias_table, ffn1_w_gu, ffn1_w_down, ln1_g, ln1_b, w_in, b_gate, conv_w_dw, conv_b_dw, conv_ln_g, conv_ln_b, w_conv_out, lambda_q1, lambda_k1, lambda_q2, lambda_k2, subln_g, w_attn_out, w_o, ln2_g, ln2_b, ffn2_w_gu, ffn2_w_down, ln3_g, ln3_b):
    p = _prepare(rel_bias_table, ffn1_w_gu, ffn1_w_down, ln1_g, ln1_b, w_in, b_gate, conv_w_dw, conv_b_dw,
                 conv_ln_g, conv_ln_b, w_conv_out, lambda_q1, lambda_k1, lambda_q2, lambda_k2, subln_g,
                 w_attn_out, w_o, ln2_g, ln2_b, ffn2_w_gu, ffn2_w_down, ln3_g, ln3_b)
    tile = min(ATTN_TILE, x_prompt.shape[1], x_sample.shape[1])
    bias = _bias_tiles(p["table"], tile)
    return (_layer(x_prompt, p, bias, tile), _layer(x_sample, p, bias, tile))
```

```python
import functools
import math

import jax
import jax.numpy as jnp
from jax import lax
from jax.experimental import pallas as pl
from jax.experimental.pallas import tpu as pltpu

F32 = jnp.float32
BF16 = jnp.bfloat16

D_MODEL = 1024
N_HEADS = 4
HEAD_DIM = 64
V_DIM = 2 * HEAD_DIM
ATTN_WIDTH = N_HEADS * 2 * HEAD_DIM
CONV_WIDTH = 512
CONV_KERNEL = 31
CONV_HALO = 16
D_FF = 2816
NUM_BUCKETS = 32
MAX_DISTANCE = 128
DEPTH = 1
ALPHA = (2.0 * DEPTH) ** 0.25
LN_EPS = 1e-5
ATTN_SCALE = HEAD_DIM ** -0.5
LAMBDA_INIT = 0.8 - 0.6 * math.exp(-0.3 * 0)
LOG2E = math.log2(math.e)
ONES_ROWS = 16
COL_U = 2 * CONV_WIDTH
COL_Q = ATTN_WIDTH
COL_K = ATTN_WIDTH
COL_V = N_HEADS * V_DIM

V7X_VMEM_LIMIT_BYTES = 56 * 1024 * 1024
FF_CHUNK = 256
TOKEN_TILE = 1024
CONV_TILE = 256
CONV_ROWS = 64
CONV_LANES = 256
SUBLANES = 8
ATTN_TILE = 1024
NEAR_TILES = 3
MAX_EXPONENT_DRIFT = 64.0


def _dot(a, b):
    return jnp.dot(a, b, preferred_element_type=F32)


def _layer_norm(z, g, b):
    mu = jnp.mean(z, axis=-1, keepdims=True)
    zc = z - mu
    var = jnp.mean(zc * zc, axis=-1, keepdims=True)
    return zc * lax.rsqrt(var + LN_EPS) * g + b


def _resident(shape):
    return pl.BlockSpec(shape, lambda *_: (0,) * len(shape), pipeline_mode=pl.Buffered(1))


def _params(*semantics):
    return pltpu.CompilerParams(dimension_semantics=semantics,
                                vmem_limit_bytes=V7X_VMEM_LIMIT_BYTES)


def _ffn_ln_kernel(x_ref, wg_ref, wu_ref, wd_ref, g_ref, b_ref, o_ref, h_ref):
    x = x_ref[...]
    xb = x.astype(BF16)
    for c in range(D_FF // FF_CHUNK):
        sl = slice(c * FF_CHUNK, (c + 1) * FF_CHUNK)
        a = _dot(xb, wg_ref[:, sl])
        u = _dot(xb, wu_ref[:, sl])
        h_ref[:, sl] = (a * jax.nn.sigmoid(a) * u).astype(BF16)
    y = _dot(h_ref[...], wd_ref[...])
    o_ref[...] = _layer_norm(ALPHA * x + 0.5 * y, g_ref[...], b_ref[...])


def _ffn_ln(x, wg, wu, wd, g, b):
    t = x.shape[0]
    tm = min(TOKEN_TILE, t)
    row = pl.BlockSpec((tm, D_MODEL), lambda i: (i, 0))
    return pl.pallas_call(
        _ffn_ln_kernel,
        grid=(t // tm,),
        in_specs=[row, _resident((D_MODEL, D_FF)), _resident((D_MODEL, D_FF)),
                  _resident((D_FF, D_MODEL)), _resident((1, D_MODEL)), _resident((1, D_MODEL))],
        out_specs=row,
        out_shape=jax.ShapeDtypeStruct((t, D_MODEL), F32),
        scratch_shapes=[pltpu.VMEM((tm, D_FF), BF16)],
        compiler_params=_params("parallel"),
        name="ffn_ln",
    )(x, wg, wu, wd, g, b)


def _in_proj_kernel(x_ref, w_ref, hg_ref, qt_ref, k_ref, vt_ref):
    xb = x_ref[...].astype(BF16)
    u = _dot(xb, w_ref[:, 0:COL_U])
    hg_ref[...] = u[:, :CONV_WIDTH] * jax.nn.sigmoid(u[:, CONV_WIDTH:])
    o = COL_U
    qt_ref[0] = (_dot(xb, w_ref[:, o:o + COL_Q]) * (ATTN_SCALE * LOG2E)).T.astype(BF16)
    o += COL_Q
    k_ref[...] = _dot(xb, w_ref[:, o:o + COL_K]).astype(BF16)
    o += COL_K
    vt_ref[0] = _dot(xb, w_ref[:, o:o + COL_V]).T.astype(BF16)


def _in_proj(x1, w_uqkv, batch, seq):
    t = x1.shape[0]
    tm = min(TOKEN_TILE, seq)
    per_seq = seq // tm
    ncol = COL_U + COL_Q + COL_K + COL_V
    return pl.pallas_call(
        _in_proj_kernel,
        grid=(t // tm,),
        in_specs=[pl.BlockSpec((tm, D_MODEL), lambda i: (i, 0)), _resident((D_MODEL, ncol))],
        out_specs=[pl.BlockSpec((tm, CONV_WIDTH), lambda i: (i, 0)),
                   pl.BlockSpec((1, ATTN_WIDTH, tm), lambda i: (i // per_seq, 0, i % per_seq)),
                   pl.BlockSpec((tm, ATTN_WIDTH), lambda i: (i, 0)),
                   pl.BlockSpec((1, COL_V, tm), lambda i: (i // per_seq, 0, i % per_seq))],
        out_shape=[jax.ShapeDtypeStruct((t, CONV_WIDTH), F32),
                   jax.ShapeDtypeStruct((batch, ATTN_WIDTH, seq), BF16),
                   jax.ShapeDtypeStruct((t, ATTN_WIDTH), BF16),
                   jax.ShapeDtypeStruct((batch, COL_V, seq), BF16)],
        compiler_params=_params("parallel"),
        name="in_proj",
    )(x1, w_uqkv)


def _conv_kernel(prev_ref, cur_ref, next_ref, w_ref, b_ref, g_ref, beta_ref, o_ref, ext_ref, *, tc):
    i = pl.program_id(1)
    last = pl.num_programs(1) - 1
    ext_ref[0:CONV_HALO, :] = jnp.where(i > 0, prev_ref[0], 0.0)
    ext_ref[CONV_HALO:CONV_HALO + tc, :] = cur_ref[0]
    ext_ref[CONV_HALO + tc:2 * CONV_HALO + tc, :] = jnp.where(i < last, next_ref[0], 0.0)
    first_tap = CONV_HALO - CONV_KERNEL // 2
    for r in range(tc // CONV_ROWS):
        r0 = r * CONV_ROWS
        halves = []
        for c in range(CONV_WIDTH // CONV_LANES):
            cs = slice(c * CONV_LANES, (c + 1) * CONV_LANES)
            acc = None
            for b in range(SUBLANES):
                u = None
                for j in range(CONV_KERNEL):
                    if (first_tap + j) % SUBLANES != b:
                        continue
                    e0 = r0 + first_tap + j - b
                    term = w_ref[j:j + 1, cs] * ext_ref[e0:e0 + CONV_ROWS + SUBLANES, cs]
                    u = term if u is None else u + term
                u = u[b:b + CONV_ROWS]
                acc = u if acc is None else acc + u
            halves.append(acc)
        y = _layer_norm(jnp.concatenate(halves, axis=1) + b_ref[...], g_ref[...], beta_ref[...])
        o_ref[0, r0:r0 + CONV_ROWS, :] = (y * jax.nn.sigmoid(y)).astype(BF16)


def _conv_branch(hg, w_dw, b_dw, ln_g, ln_b, batch, seq):
    tc = min(CONV_TILE, seq)
    hpb = tc // CONV_HALO
    nhalo = seq // CONV_HALO
    hg3 = hg.reshape(batch, seq, CONV_WIDTH)
    out = pl.pallas_call(
        functools.partial(_conv_kernel, tc=tc),
        grid=(batch, seq // tc),
        in_specs=[
            pl.BlockSpec((1, CONV_HALO, CONV_WIDTH), lambda b, i: (b, jnp.maximum(i * hpb - 1, 0), 0)),
            pl.BlockSpec((1, tc, CONV_WIDTH), lambda b, i: (b, i, 0)),
            pl.BlockSpec((1, CONV_HALO, CONV_WIDTH),
                         lambda b, i: (b, jnp.minimum((i + 1) * hpb, nhalo - 1), 0)),
            _resident((CONV_KERNEL, CONV_WIDTH)), _resident((1, CONV_WIDTH)),
            _resident((1, CONV_WIDTH)), _resident((1, CONV_WIDTH))],
        out_specs=pl.BlockSpec((1, tc, CONV_WIDTH), lambda b, i: (b, i, 0)),
        out_shape=jax.ShapeDtypeStruct((batch, seq, CONV_WIDTH), BF16),
        scratch_shapes=[pltpu.VMEM((tc + 2 * CONV_HALO, CONV_WIDTH), F32)],
        compiler_params=_params("parallel", "parallel"),
        name="conv_branch",
    )(hg3, hg3, hg3, w_dw, b_dw, ln_g, ln_b)
    return out.reshape(batch * seq, CONV_WIDTH)


def _t5_bucket(rel):
    nb = NUM_BUCKETS // 2
    ret = jnp.where(rel > 0, nb, 0)
    n = jnp.abs(rel)
    max_exact = nb // 2
    nf = jnp.maximum(n, 1).astype(F32)
    large = max_exact + (jnp.log(nf / max_exact) / math.log(MAX_DISTANCE / max_exact)
                         * (nb - max_exact)).astype(jnp.int32)
    large = jnp.minimum(large, nb - 1)
    return ret + jnp.where(n < max_exact, n, large)


def _bias_kernel(tab_ref, o_ref, *, tile):
    h = pl.program_id(0)
    d = pl.program_id(1)
    key = lax.broadcasted_iota(jnp.int32, (tile, tile), 0)
    query = lax.broadcasted_iota(jnp.int32, (tile, tile), 1)
    bucket = _t5_bucket(key - query + (d - 1) * tile)
    acc = jnp.zeros((tile, tile), F32)
    for b in range(NUM_BUCKETS):
        acc = jnp.where(bucket == b, tab_ref[b, h], acc)
    o_ref[0, 0] = acc * LOG2E


def _bias_tiles(table, tile):
    return pl.pallas_call(
        functools.partial(_bias_kernel, tile=tile),
        grid=(N_HEADS, 3),
        in_specs=[pl.BlockSpec(memory_space=pltpu.SMEM)],
        out_specs=pl.BlockSpec((1, 1, tile, tile), lambda h, d: (h, d, 0, 0)),
        out_shape=jax.ShapeDtypeStruct((N_HEADS, 3, tile, tile), F32),
        compiler_params=_params("parallel", "parallel"),
        name="bias_tiles",
    )(table)


def _attn_kernel(tab_ref, qt_ref, k_ref, vt_ref, bias_ref, lq1_ref, lk1_ref, lq2_ref, lk2_ref, g_ref,
                 o_ref, acc_ref, m_ref, s_ref, drift_ref, *, tile, nk):
    h = pl.program_id(1)
    qi = pl.program_id(2)
    qt = qt_ref[0]
    row = lax.broadcasted_iota(jnp.int32, qt.shape, 0)
    zero = jnp.zeros_like(qt)
    qt_maps = (jnp.where(row < HEAD_DIM, qt, zero), jnp.where(row >= HEAD_DIM, qt, zero))
    ones = jnp.ones((ONES_ROWS, tile), BF16)
    def keys(ki):
        k0 = pl.multiple_of(ki * tile, tile)
        vt_aug = jnp.concatenate([vt_ref[0, :, pl.ds(k0, tile)], ones], axis=0)
        return k_ref[pl.ds(k0, tile), :], vt_aug

    def logits(k, mi, bias):
        s = _dot(k, qt_maps[mi])
        return s if bias is None else s + bias

    def exact_update(stage, mi):
        ki, bias, shift = stage
        k, vt_aug = keys(ki)
        s_ref[...] = logits(k, mi, bias)
        m_old = m_ref[mi]
        m_new = jnp.maximum(m_old, jnp.max(s_ref[...], axis=0, keepdims=True) + shift)
        p = jnp.exp2(s_ref[...] - (m_new - shift)).astype(BF16)
        acc_ref[mi] = jnp.exp2(m_old - m_new) * acc_ref[mi] + _dot(vt_aug, p)
        m_ref[mi] = m_new

    def streamed_update(stage, mi):
        ki, bias, shift = stage
        k, vt_aug = keys(ki)
        s = logits(k, mi, bias)
        m_old = m_ref[mi]
        p = jnp.exp2(s - (m_old - shift)).astype(BF16)
        t_max = jnp.max(s, axis=0, keepdims=True) + shift
        m_new = jnp.maximum(m_old, t_max)
        acc_ref[mi] = jnp.exp2(m_old - m_new) * (acc_ref[mi] + _dot(vt_aug, p))
        m_ref[mi] = m_new
        drift_ref[mi] = jnp.maximum(drift_ref[mi], t_max - m_old)

    far_shift = lambda ki: LOG2E * jnp.where(ki < qi, tab_ref[NUM_BUCKETS // 2 - 1, h],
                                             tab_ref[NUM_BUCKETS - 1, h])

    w0 = jnp.clip(qi - 1, 0, nk - NEAR_TILES)

    def near_stage(pos):
        ki = w0 + pos
        d = ki - qi
        banded = jnp.abs(d) <= 1
        bias = jnp.where(banded, bias_ref[0, jnp.clip(d, -1, 1) + 1], 0.0)
        return ki, bias, jnp.where(banded, 0.0, far_shift(ki))

    def far_stage(j):
        ki = jnp.where(j < w0, j, j + NEAR_TILES)
        return ki, None, far_shift(ki)

    def reset():
        m_ref[...] = jnp.full(m_ref.shape, -jnp.inf, F32)
        acc_ref[...] = jnp.zeros(acc_ref.shape, F32)

    def sweep(update, stage_of, start, stop):
        def body(i, carry):
            stage = stage_of(i)
            for mi in range(2):
                update(stage, mi)
            return carry

        lax.fori_loop(start, stop, body, 0)

    n_far = nk - NEAR_TILES
    reset()
    drift_ref[...] = jnp.zeros(drift_ref.shape, F32)
    sweep(exact_update, near_stage, 0, 1)
    sweep(streamed_update, near_stage, 1, NEAR_TILES)
    sweep(streamed_update, far_stage, 0, n_far)

    @pl.when(jnp.max(drift_ref[...]) > MAX_EXPONENT_DRIFT)
    def _():
        reset()
        sweep(exact_update, near_stage, 0, NEAR_TILES)
        sweep(exact_update, far_stage, 0, n_far)


    lam = (jnp.exp(jnp.sum(lq1_ref[...] * lk1_ref[...], keepdims=True))
           - jnp.exp(jnp.sum(lq2_ref[...] * lk2_ref[...], keepdims=True)) + LAMBDA_INIT)
    o1 = acc_ref[0]
    o2 = acc_ref[1]
    a = o1[:V_DIM] / o1[V_DIM:V_DIM + 1] - lam * (o2[:V_DIM] / o2[V_DIM:V_DIM + 1])
    r = a * lax.rsqrt(jnp.mean(a * a, axis=0, keepdims=True) + LN_EPS) * g_ref[...]
    o_ref[...] = (r * (1.0 - LAMBDA_INIT)).T.astype(BF16)


def _attention(qt, k, vt, table, bias, lq1, lk1, lq2, lk2, subln_g, batch, seq, tile):
    nk = seq // tile
    assert nk >= NEAR_TILES and tile >= MAX_DISTANCE
    small = _resident((1, HEAD_DIM))
    return pl.pallas_call(
        functools.partial(_attn_kernel, tile=tile, nk=nk),
        grid=(batch, N_HEADS, nk),
        in_specs=[
            pl.BlockSpec(memory_space=pltpu.SMEM),
            pl.BlockSpec((1, V_DIM, tile), lambda b, h, i: (b, h, i)),
            pl.BlockSpec((seq, V_DIM), lambda b, h, i: (b, h)),
            pl.BlockSpec((1, V_DIM, seq), lambda b, h, i: (b, h, 0)),
            pl.BlockSpec((1, 3, tile, tile), lambda b, h, i: (h, 0, 0, 0), pipeline_mode=pl.Buffered(1)),
            small, small, small, small, _resident((V_DIM, 1))],
        out_specs=pl.BlockSpec((tile, V_DIM), lambda b, h, i: (b * nk + i, h)),
        out_shape=jax.ShapeDtypeStruct((batch * seq, ATTN_WIDTH), BF16),
        scratch_shapes=[pltpu.VMEM((2, V_DIM + ONES_ROWS, tile), F32), pltpu.VMEM((2, 1, tile), F32),
                        pltpu.VMEM((tile, tile), F32), pltpu.VMEM((2, 1, tile), F32)],
        compiler_params=_params("parallel", "parallel", "parallel"),
        name="diff_attention",
    )(table, qt, k, vt, bias, lq1, lk1, lq2, lk2, subln_g)


def _merge_ln_kernel(x_ref, hc_ref, att_ref, wco_ref, wao_ref, wg_ref, bg_ref, wo_ref, g_ref, b_ref, o_ref):
    x = x_ref[...]
    xb = x.astype(BF16)
    y_conv = _dot(hc_ref[...], wco_ref[...])
    y_attn = _dot(att_ref[...], wao_ref[...])
    g_conv = jax.nn.sigmoid(_dot(xb, wg_ref[:, :D_MODEL]) + bg_ref[:, :D_MODEL])
    g_attn = jax.nn.sigmoid(_dot(xb, wg_ref[:, D_MODEL:]) + bg_ref[:, D_MODEL:])
    merged = g_conv * y_conv + g_attn * y_attn
    m = _dot(merged.astype(BF16), wo_ref[...])
    o_ref[...] = _layer_norm(ALPHA * x + m, g_ref[...], b_ref[...])


def _merge_ln(x1, hc, att, wco, wao, wg, bg, wo, g, b):
    t = x1.shape[0]
    tm = min(TOKEN_TILE, t)
    row = pl.BlockSpec((tm, D_MODEL), lambda i: (i, 0))
    half = pl.BlockSpec((tm, CONV_WIDTH), lambda i: (i, 0))
    return pl.pallas_call(
        _merge_ln_kernel,
        grid=(t // tm,),
        in_specs=[row, half, half, _resident((CONV_WIDTH, D_MODEL)), _resident((ATTN_WIDTH, D_MODEL)),
                  _resident((D_MODEL, 2 * D_MODEL)), _resident((1, 2 * D_MODEL)),
                  _resident((D_MODEL, D_MODEL)), _resident((1, D_MODEL)), _resident((1, D_MODEL))],
        out_specs=row,
        out_shape=jax.ShapeDtypeStruct((t, D_MODEL), F32),
        compiler_params=_params("parallel"),
        name="merge_ln",
    )(x1, hc, att, wco, wao, wg, bg, wo, g, b)


def _layer(x, p, bias, tile):
    batch, seq, _ = x.shape
    x0 = x.reshape(batch * seq, D_MODEL)
    x1 = _ffn_ln(x0, p["ffn1_wg"], p["ffn1_wu"], p["ffn1_wd"], p["ln1_g"], p["ln1_b"])
    hg, qt, k, vt = _in_proj(x1, p["w_uqkv"], batch, seq)
    hc = _conv_branch(hg, p["conv_w"], p["conv_b"], p["conv_ln_g"], p["conv_ln_b"], batch, seq)
    att = _attention(qt, k, vt, p["table"], bias, p["lq1"], p["lk1"], p["lq2"], p["lk2"], p["subln_g"],
                     batch, seq, tile)
    x2 = _merge_ln(x1, hc, att, p["w_conv_out"], p["w_attn_out"], p["w_gate"], p["b_gate"], p["w_o"],
                   p["ln2_g"], p["ln2_b"])
    x3 = _ffn_ln(x2, p["ffn2_wg"], p["ffn2_wu"], p["ffn2_wd"], p["ln3_g"], p["ln3_b"])
    return x3.reshape(batch, seq, D_MODEL)


def _prepare(rel_bias_table, ffn1_w_gu, ffn1_w_down, ln1_g, ln1_b, w_in, b_gate, conv_w_dw, conv_b_dw,
             conv_ln_g, conv_ln_b, w_conv_out, lambda_q1, lambda_k1, lambda_q2, lambda_k2, subln_g,
             w_attn_out, w_o, ln2_g, ln2_b, ffn2_w_gu, ffn2_w_down, ln3_g, ln3_b):
    l = 0
    n_uqkv = COL_U + COL_Q + COL_K + COL_V
    row = lambda a: a[l].reshape(1, -1).astype(F32)
    return {
        "table": rel_bias_table.astype(F32),
        "ffn1_wg": ffn1_w_gu[l, :, :D_FF].astype(BF16), "ffn1_wu": ffn1_w_gu[l, :, D_FF:].astype(BF16),
        "ffn1_wd": ffn1_w_down[l].astype(BF16), "ln1_g": row(ln1_g), "ln1_b": row(ln1_b),
        "w_uqkv": w_in[l, :, :n_uqkv].astype(BF16), "w_gate": w_in[l, :, n_uqkv:].astype(BF16),
        "b_gate": row(b_gate),
        "conv_w": conv_w_dw[l].reshape(CONV_KERNEL, CONV_WIDTH).astype(F32), "conv_b": row(conv_b_dw),
        "conv_ln_g": row(conv_ln_g), "conv_ln_b": row(conv_ln_b),
        "w_conv_out": w_conv_out[l].astype(BF16),
        "lq1": row(lambda_q1), "lk1": row(lambda_k1), "lq2": row(lambda_q2), "lk2": row(lambda_k2),
        "subln_g": subln_g[l].reshape(-1, 1).astype(F32), "w_attn_out": w_attn_out[l].astype(BF16), "w_o": w_o[l].astype(BF16),
        "ln2_g": row(ln2_g), "ln2_b": row(ln2_b),
        "ffn2_wg": ffn2_w_gu[l, :, :D_FF].astype(BF16), "ffn2_wu": ffn2_w_gu[l, :, D_FF:].astype(BF16),
        "ffn2_wd": ffn2_w_down[l].astype(BF16), "ln3_g": row(ln3_g), "ln3_b": row(ln3_b),
    }


def kernel(x_prompt, x_sample, rel_bias_table, ffn1_w_gu, ffn1_w_down, ln1_g, ln1_b, w_in, b_gate, conv_w_dw, conv_b_dw, conv_ln_g, conv_ln_b, w_conv_out, lambda_q1, lambda_k1, lambda_q2, lambda_k2, subln_g, w_attn_out, w_o, ln2_g, ln2_b, ffn2_w_gu, ffn2_w_down, ln3_g, ln3_b):
    p = _prepare(rel_bias_table, ffn1_w_gu, ffn1_w_down, ln1_g, ln1_b, w_in, b_gate, conv_w_dw, conv_b_dw,
                 conv_ln_g, conv_ln_b, w_conv_out, lambda_q1, lambda_k1, lambda_q2, lambda_k2, subln_g,
                 w_attn_out, w_o, ln2_g, ln2_b, ffn2_w_gu, ffn2_w_down, ln3_g, ln3_b)
    tile = min(ATTN_TILE, x_prompt.shape[1], x_sample.shape[1])
    bias = _bias_tiles(p["table"], tile)
    return (_layer(x_prompt, p, bias, tile), _layer(x_sample, p, bias, tile))
```

```python
import functools
import math

import jax
import jax.numpy as jnp
from jax import lax
from jax.experimental import pallas as pl
from jax.experimental.pallas import tpu as pltpu

F32 = jnp.float32
BF16 = jnp.bfloat16

D_MODEL = 1024
N_HEADS = 4
HEAD_DIM = 64
V_DIM = 2 * HEAD_DIM
ATTN_WIDTH = N_HEADS * 2 * HEAD_DIM
CONV_WIDTH = 512
CONV_KERNEL = 31
CONV_HALO = 16
D_FF = 2816
NUM_BUCKETS = 32
MAX_DISTANCE = 128
DEPTH = 1
ALPHA = (2.0 * DEPTH) ** 0.25
LN_EPS = 1e-5
ATTN_SCALE = HEAD_DIM ** -0.5
LAMBDA_INIT = 0.8 - 0.6 * math.exp(-0.3 * 0)
LOG2E = math.log2(math.e)
ONES_ROWS = 16
COL_U = 2 * CONV_WIDTH
COL_Q = ATTN_WIDTH
COL_K = ATTN_WIDTH
COL_V = N_HEADS * V_DIM

V7X_VMEM_LIMIT_BYTES = 56 * 1024 * 1024
FF_CHUNK = 256
TOKEN_TILE = 1024
CONV_TILE = 256
CONV_ROWS = 64
CONV_LANES = 256
SUBLANES = 8
ATTN_TILE = 1024
NEAR_TILES = 3
MAX_EXPONENT_DRIFT = 64.0
PROBE_KEYS = 16
BIAS_BLOCK = 128


def _dot(a, b):
    return jnp.dot(a, b, preferred_element_type=F32)


def _layer_norm(z, g, b):
    mu = jnp.mean(z, axis=-1, keepdims=True)
    zc = z - mu
    var = jnp.mean(zc * zc, axis=-1, keepdims=True)
    return zc * lax.rsqrt(var + LN_EPS) * g + b


def _resident(shape):
    return pl.BlockSpec(shape, lambda *_: (0,) * len(shape), pipeline_mode=pl.Buffered(1))


def _params(*semantics):
    return pltpu.CompilerParams(dimension_semantics=semantics,
                                vmem_limit_bytes=V7X_VMEM_LIMIT_BYTES)


def _ffn_ln_kernel(x_ref, wg_ref, wu_ref, wd_ref, g_ref, b_ref, o_ref, h_ref):
    x = x_ref[...]
    xb = x.astype(BF16)
    for c in range(D_FF // FF_CHUNK):
        sl = slice(c * FF_CHUNK, (c + 1) * FF_CHUNK)
        a = _dot(xb, wg_ref[:, sl])
        u = _dot(xb, wu_ref[:, sl])
        h_ref[:, sl] = (a * jax.nn.sigmoid(a) * u).astype(BF16)
    y = _dot(h_ref[...], wd_ref[...])
    o_ref[...] = _layer_norm(ALPHA * x + 0.5 * y, g_ref[...], b_ref[...])


def _ffn_ln(x, wg, wu, wd, g, b):
    t = x.shape[0]
    tm = min(TOKEN_TILE, t)
    row = pl.BlockSpec((tm, D_MODEL), lambda i: (i, 0))
    return pl.pallas_call(
        _ffn_ln_kernel,
        grid=(t // tm,),
        in_specs=[row, _resident((D_MODEL, D_FF)), _resident((D_MODEL, D_FF)),
                  _resident((D_FF, D_MODEL)), _resident((1, D_MODEL)), _resident((1, D_MODEL))],
        out_specs=row,
        out_shape=jax.ShapeDtypeStruct((t, D_MODEL), F32),
        scratch_shapes=[pltpu.VMEM((tm, D_FF), BF16)],
        compiler_params=_params("parallel"),
        name="ffn_ln",
    )(x, wg, wu, wd, g, b)


def _in_proj_kernel(x_ref, w_ref, hg_ref, qt_ref, k_ref, vt_ref):
    xb = x_ref[...].astype(BF16)
    u = _dot(xb, w_ref[:, 0:COL_U])
    hg_ref[...] = u[:, :CONV_WIDTH] * jax.nn.sigmoid(u[:, CONV_WIDTH:])
    o = COL_U
    qt_ref[0] = (_dot(xb, w_ref[:, o:o + COL_Q]) * (ATTN_SCALE * LOG2E)).T.astype(BF16)
    o += COL_Q
    k_ref[...] = _dot(xb, w_ref[:, o:o + COL_K]).astype(BF16)
    o += COL_K
    vt_ref[0] = _dot(xb, w_ref[:, o:o + COL_V]).T.astype(BF16)


def _in_proj(x1, w_uqkv, batch, seq):
    t = x1.shape[0]
    tm = min(TOKEN_TILE, seq)
    per_seq = seq // tm
    ncol = COL_U + COL_Q + COL_K + COL_V
    return pl.pallas_call(
        _in_proj_kernel,
        grid=(t // tm,),
        in_specs=[pl.BlockSpec((tm, D_MODEL), lambda i: (i, 0)), _resident((D_MODEL, ncol))],
        out_specs=[pl.BlockSpec((tm, CONV_WIDTH), lambda i: (i, 0)),
                   pl.BlockSpec((1, ATTN_WIDTH, tm), lambda i: (i // per_seq, 0, i % per_seq)),
                   pl.BlockSpec((tm, ATTN_WIDTH), lambda i: (i, 0)),
                   pl.BlockSpec((1, COL_V, tm), lambda i: (i // per_seq, 0, i % per_seq))],
        out_shape=[jax.ShapeDtypeStruct((t, CONV_WIDTH), F32),
                   jax.ShapeDtypeStruct((batch, ATTN_WIDTH, seq), BF16),
                   jax.ShapeDtypeStruct((t, ATTN_WIDTH), BF16),
                   jax.ShapeDtypeStruct((batch, COL_V, seq), BF16)],
        compiler_params=_params("parallel"),
        name="in_proj",
    )(x1, w_uqkv)


def _conv_kernel(prev_ref, cur_ref, next_ref, w_ref, b_ref, g_ref, beta_ref, o_ref, ext_ref, *, tc):
    i = pl.program_id(1)
    last = pl.num_programs(1) - 1
    ext_ref[0:CONV_HALO, :] = jnp.where(i > 0, prev_ref[0], 0.0)
    ext_ref[CONV_HALO:CONV_HALO + tc, :] = cur_ref[0]
    ext_ref[CONV_HALO + tc:2 * CONV_HALO + tc, :] = jnp.where(i < last, next_ref[0], 0.0)
    first_tap = CONV_HALO - CONV_KERNEL // 2
    for r in range(tc // CONV_ROWS):
        r0 = r * CONV_ROWS
        halves = []
        for c in range(CONV_WIDTH // CONV_LANES):
            cs = slice(c * CONV_LANES, (c + 1) * CONV_LANES)
            acc = None
            for b in range(SUBLANES):
                u = None
                for j in range(CONV_KERNEL):
                    if (first_tap + j) % SUBLANES != b:
                        continue
                    e0 = r0 + first_tap + j - b
                    term = w_ref[j:j + 1, cs] * ext_ref[e0:e0 + CONV_ROWS + SUBLANES, cs]
                    u = term if u is None else u + term
                u = u[b:b + CONV_ROWS]
                acc = u if acc is None else acc + u
            halves.append(acc)
        y = _layer_norm(jnp.concatenate(halves, axis=1) + b_ref[...], g_ref[...], beta_ref[...])
        o_ref[0, r0:r0 + CONV_ROWS, :] = (y * jax.nn.sigmoid(y)).astype(BF16)


def _conv_branch(hg, w_dw, b_dw, ln_g, ln_b, batch, seq):
    tc = min(CONV_TILE, seq)
    hpb = tc // CONV_HALO
    nhalo = seq // CONV_HALO
    hg3 = hg.reshape(batch, seq, CONV_WIDTH)
    out = pl.pallas_call(
        functools.partial(_conv_kernel, tc=tc),
        grid=(batch, seq // tc),
        in_specs=[
            pl.BlockSpec((1, CONV_HALO, CONV_WIDTH), lambda b, i: (b, jnp.maximum(i * hpb - 1, 0), 0)),
            pl.BlockSpec((1, tc, CONV_WIDTH), lambda b, i: (b, i, 0)),
            pl.BlockSpec((1, CONV_HALO, CONV_WIDTH),
                         lambda b, i: (b, jnp.minimum((i + 1) * hpb, nhalo - 1), 0)),
            _resident((CONV_KERNEL, CONV_WIDTH)), _resident((1, CONV_WIDTH)),
            _resident((1, CONV_WIDTH)), _resident((1, CONV_WIDTH))],
        out_specs=pl.BlockSpec((1, tc, CONV_WIDTH), lambda b, i: (b, i, 0)),
        out_shape=jax.ShapeDtypeStruct((batch, seq, CONV_WIDTH), BF16),
        scratch_shapes=[pltpu.VMEM((tc + 2 * CONV_HALO, CONV_WIDTH), F32)],
        compiler_params=_params("parallel", "parallel"),
        name="conv_branch",
    )(hg3, hg3, hg3, w_dw, b_dw, ln_g, ln_b)
    return out.reshape(batch * seq, CONV_WIDTH)


def _t5_bucket(rel):
    nb = NUM_BUCKETS // 2
    ret = jnp.where(rel > 0, nb, 0)
    n = jnp.abs(rel)
    max_exact = nb // 2
    nf = jnp.maximum(n, 1).astype(F32)
    large = max_exact + (jnp.log(nf / max_exact) / math.log(MAX_DISTANCE / max_exact)
                         * (nb - max_exact)).astype(jnp.int32)
    large = jnp.minimum(large, nb - 1)
    return ret + jnp.where(n < max_exact, n, large)


def _bias_kernel(tab_ref, o_ref, *, tile):
    h = pl.program_id(0)
    d = pl.program_id(1)
    nb = tile // BIAS_BLOCK
    key = lax.broadcasted_iota(jnp.int32, (BIAS_BLOCK, BIAS_BLOCK), 0)
    query = lax.broadcasted_iota(jnp.int32, (BIAS_BLOCK, BIAS_BLOCK), 1)

    def banded_block(o):
        bucket = _t5_bucket(key - query + o * BIAS_BLOCK)
        acc = jnp.zeros((BIAS_BLOCK, BIAS_BLOCK), F32)
        for b in range(NUM_BUCKETS):
            acc = jnp.where(bucket == b, tab_ref[b, h], acc)
        return acc * LOG2E

    below, diag, above = banded_block(-1), banded_block(0), banded_block(1)
    far_below = tab_ref[NUM_BUCKETS // 2 - 1, h] * LOG2E
    far_above = tab_ref[NUM_BUCKETS - 1, h] * LOG2E
    for kb in range(nb):
        for qb in range(nb):
            o = kb - qb + nb * (d - 1)
            block = jnp.where(o == -1, below, jnp.where(o == 0, diag, above))
            block = jnp.where(o <= -2, far_below, jnp.where(o >= 2, far_above, block))
            o_ref[0, 0, kb * BIAS_BLOCK:(kb + 1) * BIAS_BLOCK, qb * BIAS_BLOCK:(qb + 1) * BIAS_BLOCK] = block


def _bias_tiles(table, tile):
    return pl.pallas_call(
        functools.partial(_bias_kernel, tile=tile),
        grid=(N_HEADS, 3),
        in_specs=[pl.BlockSpec(memory_space=pltpu.SMEM)],
        out_specs=pl.BlockSpec((1, 1, tile, tile), lambda h, d: (h, d, 0, 0)),
        out_shape=jax.ShapeDtypeStruct((N_HEADS, 3, tile, tile), F32),
        compiler_params=_params("parallel", "parallel"),
        name="bias_tiles",
    )(table)


def _attn_kernel(tab_ref, qt_ref, k_ref, vt_ref, bias_ref, lq1_ref, lk1_ref, lq2_ref, lk2_ref, g_ref,
                 o_ref, acc_ref, m_ref, s_ref, drift_ref, *, tile, nk):
    h = pl.program_id(1)
    qi = pl.program_id(2)
    qt = qt_ref[0]
    row = lax.broadcasted_iota(jnp.int32, qt.shape, 0)
    zero = jnp.zeros_like(qt)
    qt_maps = (jnp.where(row < HEAD_DIM, qt, zero), jnp.where(row >= HEAD_DIM, qt, zero))
    ones = jnp.ones((ONES_ROWS, tile), BF16)
    def keys(ki):
        k0 = pl.multiple_of(ki * tile, tile)
        vt_aug = jnp.concatenate([vt_ref[0, :, pl.ds(k0, tile)], ones], axis=0)
        return k_ref[pl.ds(k0, tile), :], vt_aug

    def logits(k, mi, bias):
        s = _dot(k, qt_maps[mi])
        return s if bias is None else s + bias

    def exact_update(stage, mi):
        ki, bias, shift = stage
        k, vt_aug = keys(ki)
        s_ref[...] = logits(k, mi, bias)
        m_old = m_ref[mi]
        m_new = jnp.maximum(m_old, jnp.max(s_ref[...], axis=0, keepdims=True) + shift)
        p = jnp.exp2(s_ref[...] - (m_new - shift)).astype(BF16)
        acc_ref[mi] = jnp.exp2(m_old - m_new) * acc_ref[mi] + _dot(vt_aug, p)
        m_ref[mi] = m_new

    def streamed_update(stage, mi, first=False):
        ki, bias, shift = stage
        k, vt_aug = keys(ki)
        s = logits(k, mi, bias)
        m_old = jnp.max(s[:PROBE_KEYS], axis=0, keepdims=True) + shift if first else m_ref[mi]
        p = jnp.exp2(s - (m_old - shift)).astype(BF16)
        t_max = jnp.max(s, axis=0, keepdims=True) + shift
        m_new = jnp.maximum(m_old, t_max)
        acc_ref[mi] = jnp.exp2(m_old - m_new) * (acc_ref[mi] + _dot(vt_aug, p))
        m_ref[mi] = m_new
        drift_ref[mi] = jnp.maximum(drift_ref[mi], t_max - m_old)

    far_shift = lambda ki: LOG2E * jnp.where(ki < qi, tab_ref[NUM_BUCKETS // 2 - 1, h],
                                             tab_ref[NUM_BUCKETS - 1, h])

    w0 = jnp.clip(qi - 1, 0, nk - NEAR_TILES)

    def near_stage(pos):
        ki = w0 + pos
        d = ki - qi
        banded = jnp.abs(d) <= 1
        bias = jnp.where(banded, bias_ref[0, jnp.clip(d, -1, 1) + 1], 0.0)
        return ki, bias, jnp.where(banded, 0.0, far_shift(ki))

    def far_stage(j):
        ki = jnp.where(j < w0, j, j + NEAR_TILES)
        return ki, None, far_shift(ki)

    def reset():
        m_ref[...] = jnp.full(m_ref.shape, -jnp.inf, F32)
        acc_ref[...] = jnp.zeros(acc_ref.shape, F32)

    def sweep(update, stage_of, start, stop):
        def body(i, carry):
            stage = stage_of(i)
            for mi in range(2):
                update(stage, mi)
            return carry

        lax.fori_loop(start, stop, body, 0)

    n_far = nk - NEAR_TILES
    reset()
    drift_ref[...] = jnp.zeros(drift_ref.shape, F32)
    sweep(functools.partial(streamed_update, first=True), near_stage, 0, 1)
    sweep(streamed_update, near_stage, 1, NEAR_TILES)
    sweep(streamed_update, far_stage, 0, n_far)

    @pl.when(jnp.max(drift_ref[...]) > MAX_EXPONENT_DRIFT)
    def _():
        reset()
        sweep(exact_update, near_stage, 0, NEAR_TILES)
        sweep(exact_update, far_stage, 0, n_far)


    lam = (jnp.exp(jnp.sum(lq1_ref[...] * lk1_ref[...], keepdims=True))
           - jnp.exp(jnp.sum(lq2_ref[...] * lk2_ref[...], keepdims=True)) + LAMBDA_INIT)
    o1 = acc_ref[0]
    o2 = acc_ref[1]
    a = o1[:V_DIM] / o1[V_DIM:V_DIM + 1] - lam * (o2[:V_DIM] / o2[V_DIM:V_DIM + 1])
    r = a * lax.rsqrt(jnp.mean(a * a, axis=0, keepdims=True) + LN_EPS) * g_ref[...]
    o_ref[...] = (r * (1.0 - LAMBDA_INIT)).T.astype(BF16)


def _attention(qt, k, vt, table, bias, lq1, lk1, lq2, lk2, subln_g, batch, seq, tile):
    nk = seq // tile
    assert nk >= NEAR_TILES and tile % BIAS_BLOCK == 0 and BIAS_BLOCK >= MAX_DISTANCE
    small = _resident((1, HEAD_DIM))
    return pl.pallas_call(
        functools.partial(_attn_kernel, tile=tile, nk=nk),
        grid=(batch, N_HEADS, nk),
        in_specs=[
            pl.BlockSpec(memory_space=pltpu.SMEM),
            pl.BlockSpec((1, V_DIM, tile), lambda b, h, i: (b, h, i)),
            pl.BlockSpec((seq, V_DIM), lambda b, h, i: (b, h)),
            pl.BlockSpec((1, V_DIM, seq), lambda b, h, i: (b, h, 0)),
            pl.BlockSpec((1, 3, tile, tile), lambda b, h, i: (h, 0, 0, 0), pipeline_mode=pl.Buffered(1)),
            small, small, small, small, _resident((V_DIM, 1))],
        out_specs=pl.BlockSpec((tile, V_DIM), lambda b, h, i: (b * nk + i, h)),
        out_shape=jax.ShapeDtypeStruct((batch * seq, ATTN_WIDTH), BF16),
        scratch_shapes=[pltpu.VMEM((2, V_DIM + ONES_ROWS, tile), F32), pltpu.VMEM((2, 1, tile), F32),
                        pltpu.VMEM((tile, tile), F32), pltpu.VMEM((2, 1, tile), F32)],
        compiler_params=_params("parallel", "parallel", "parallel"),
        name="diff_attention",
    )(table, qt, k, vt, bias, lq1, lk1, lq2, lk2, subln_g)


def _merge_ln_kernel(x_ref, hc_ref, att_ref, wco_ref, wao_ref, wg_ref, bg_ref, wo_ref, g_ref, b_ref, o_ref):
    x = x_ref[...]
    xb = x.astype(BF16)
    y_conv = _dot(hc_ref[...], wco_ref[...])
    y_attn = _dot(att_ref[...], wao_ref[...])
    g_conv = jax.nn.sigmoid(_dot(xb, wg_ref[:, :D_MODEL]) + bg_ref[:, :D_MODEL])
    g_attn = jax.nn.sigmoid(_dot(xb, wg_ref[:, D_MODEL:]) + bg_ref[:, D_MODEL:])
    merged = g_conv * y_conv + g_attn * y_attn
    m = _dot(merged.astype(BF16), wo_ref[...])
    o_ref[...] = _layer_norm(ALPHA * x + m, g_ref[...], b_ref[...])


def _merge_ln(x1, hc, att, wco, wao, wg, bg, wo, g, b):
    t = x1.shape[0]
    tm = min(TOKEN_TILE, t)
    row = pl.BlockSpec((tm, D_MODEL), lambda i: (i, 0))
    half = pl.BlockSpec((tm, CONV_WIDTH), lambda i: (i, 0))
    return pl.pallas_call(
        _merge_ln_kernel,
        grid=(t // tm,),
        in_specs=[row, half, half, _resident((CONV_WIDTH, D_MODEL)), _resident((ATTN_WIDTH, D_MODEL)),
                  _resident((D_MODEL, 2 * D_MODEL)), _resident((1, 2 * D_MODEL)),
                  _resident((D_MODEL, D_MODEL)), _resident((1, D_MODEL)), _resident((1, D_MODEL))],
        out_specs=row,
        out_shape=jax.ShapeDtypeStruct((t, D_MODEL), F32),
        compiler_params=_params("parallel"),
        name="merge_ln",
    )(x1, hc, att, wco, wao, wg, bg, wo, g, b)


def _layer(x, p, bias, tile):
    batch, seq, _ = x.shape
    x0 = x.reshape(batch * seq, D_MODEL)
    x1 = _ffn_ln(x0, p["ffn1_wg"], p["ffn1_wu"], p["ffn1_wd"], p["ln1_g"], p["ln1_b"])
    hg, qt, k, vt = _in_proj(x1, p["w_uqkv"], batch, seq)
    hc = _conv_branch(hg, p["conv_w"], p["conv_b"], p["conv_ln_g"], p["conv_ln_b"], batch, seq)
    att = _attention(qt, k, vt, p["table"], bias, p["lq1"], p["lk1"], p["lq2"], p["lk2"], p["subln_g"],
                     batch, seq, tile)
    x2 = _merge_ln(x1, hc, att, p["w_conv_out"], p["w_attn_out"], p["w_gate"], p["b_gate"], p["w_o"],
                   p["ln2_g"], p["ln2_b"])
    x3 = _ffn_ln(x2, p["ffn2_wg"], p["ffn2_wu"], p["ffn2_wd"], p["ln3_g"], p["ln3_b"])
    return x3.reshape(batch, seq, D_MODEL)


def _prepare(rel_bias_table, ffn1_w_gu, ffn1_w_down, ln1_g, ln1_b, w_in, b_gate, conv_w_dw, conv_b_dw,
             conv_ln_g, conv_ln_b, w_conv_out, lambda_q1, lambda_k1, lambda_q2, lambda_k2, subln_g,
             w_attn_out, w_o, ln2_g, ln2_b, ffn2_w_gu, ffn2_w_down, ln3_g, ln3_b):
    l = 0
    n_uqkv = COL_U + COL_Q + COL_K + COL_V
    row = lambda a: a[l].reshape(1, -1).astype(F32)
    return {
        "table": rel_bias_table.astype(F32),
        "ffn1_wg": ffn1_w_gu[l, :, :D_FF].astype(BF16), "ffn1_wu": ffn1_w_gu[l, :, D_FF:].astype(BF16),
        "ffn1_wd": ffn1_w_down[l].astype(BF16), "ln1_g": row(ln1_g), "ln1_b": row(ln1_b),
        "w_uqkv": w_in[l, :, :n_uqkv].astype(BF16), "w_gate": w_in[l, :, n_uqkv:].astype(BF16),
        "b_gate": row(b_gate),
        "conv_w": conv_w_dw[l].reshape(CONV_KERNEL, CONV_WIDTH).astype(F32), "conv_b": row(conv_b_dw),
        "conv_ln_g": row(conv_ln_g), "conv_ln_b": row(conv_ln_b),
        "w_conv_out": w_conv_out[l].astype(BF16),
        "lq1": row(lambda_q1), "lk1": row(lambda_k1), "lq2": row(lambda_q2), "lk2": row(lambda_k2),
        "subln_g": subln_g[l].reshape(-1, 1).astype(F32), "w_attn_out": w_attn_out[l].astype(BF16), "w_o": w_o[l].astype(BF16),
        "ln2_g": row(ln2_g), "ln2_b": row(ln2_b),
        "ffn2_wg": ffn2_w_gu[l, :, :D_FF].astype(BF16), "ffn2_wu": ffn2_w_gu[l, :, D_FF:].astype(BF16),
        "ffn2_wd": ffn2_w_down[l].astype(BF16), "ln3_g": row(ln3_g), "ln3_b": row(ln3_b),
    }


def kernel(x_prompt, x_sample, rel_bias_table, ffn1_w_gu, ffn1_w_down, ln1_g, ln1_b, w_in, b_gate, conv_w_dw, conv_b_dw, conv_ln_g, conv_ln_b, w_conv_out, lambda_q1, lambda_k1, lambda_q2, lambda_k2, subln_g, w_attn_out, w_o, ln2_g, ln2_b, ffn2_w_gu, ffn2_w_down, ln3_g, ln3_b):
    p = _prepare(rel_bias_table, ffn1_w_gu, ffn1_w_down, ln1_g, ln1_b, w_in, b_gate, conv_w_dw, conv_b_dw,
                 conv_ln_g, conv_ln_b, w_conv_out, lambda_q1, lambda_k1, lambda_q2, lambda_k2, subln_g,
                 w_attn_out, w_o, ln2_g, ln2_b, ffn2_w_gu, ffn2_w_down, ln3_g, ln3_b)
    tile = min(ATTN_TILE, x_prompt.shape[1], x_sample.shape[1])
    bias = _bias_tiles(p["table"], tile)
    return (_layer(x_prompt, p, bias, tile), _layer(x_sample, p, bias, tile))
```

```python
import functools
import math

import jax
import jax.numpy as jnp
from jax import lax
from jax.experimental import pallas as pl
from jax.experimental.pallas import tpu as pltpu

F32 = jnp.float32
BF16 = jnp.bfloat16

D_MODEL = 1024
N_HEADS = 4
HEAD_DIM = 64
V_DIM = 2 * HEAD_DIM
ATTN_WIDTH = N_HEADS * 2 * HEAD_DIM
CONV_WIDTH = 512
CONV_KERNEL = 31
CONV_HALO = 16
D_FF = 2816
NUM_BUCKETS = 32
MAX_DISTANCE = 128
DEPTH = 1
ALPHA = (2.0 * DEPTH) ** 0.25
LN_EPS = 1e-5
ATTN_SCALE = HEAD_DIM ** -0.5
LAMBDA_INIT = 0.8 - 0.6 * math.exp(-0.3 * 0)
LOG2E = math.log2(math.e)
ONES_ROWS = 16
COL_U = 2 * CONV_WIDTH
COL_Q = ATTN_WIDTH
COL_K = ATTN_WIDTH
COL_V = N_HEADS * V_DIM

V7X_VMEM_LIMIT_BYTES = 56 * 1024 * 1024
FF_CHUNK = 256
TOKEN_TILE = 1024
CONV_TILE = 256
CONV_ROWS = 64
CONV_LANES = 256
SUBLANES = 8
ATTN_TILE = 1024
MAX_EXPONENT_DRIFT = 64.0
PROBE_KEYS = 16
BIAS_BLOCK = 128


def _dot(a, b):
    return jnp.dot(a, b, preferred_element_type=F32)


def _layer_norm(z, g, b):
    mu = jnp.mean(z, axis=-1, keepdims=True)
    zc = z - mu
    var = jnp.mean(zc * zc, axis=-1, keepdims=True)
    return zc * lax.rsqrt(var + LN_EPS) * g + b


def _resident(shape):
    return pl.BlockSpec(shape, lambda *_: (0,) * len(shape), pipeline_mode=pl.Buffered(1))


def _params(*semantics):
    return pltpu.CompilerParams(dimension_semantics=semantics,
                                vmem_limit_bytes=V7X_VMEM_LIMIT_BYTES)


def _ffn_ln_kernel(x_ref, wg_ref, wu_ref, wd_ref, g_ref, b_ref, o_ref, h_ref):
    x = x_ref[...]
    xb = x.astype(BF16)
    for c in range(D_FF // FF_CHUNK):
        sl = slice(c * FF_CHUNK, (c + 1) * FF_CHUNK)
        a = _dot(xb, wg_ref[:, sl])
        u = _dot(xb, wu_ref[:, sl])
        h_ref[:, sl] = (a * jax.nn.sigmoid(a) * u).astype(BF16)
    y = _dot(h_ref[...], wd_ref[...])
    o_ref[...] = _layer_norm(ALPHA * x + 0.5 * y, g_ref[...], b_ref[...])


def _ffn_ln(x, wg, wu, wd, g, b):
    t = x.shape[0]
    tm = min(TOKEN_TILE, t)
    row = pl.BlockSpec((tm, D_MODEL), lambda i: (i, 0))
    return pl.pallas_call(
        _ffn_ln_kernel,
        grid=(t // tm,),
        in_specs=[row, _resident((D_MODEL, D_FF)), _resident((D_MODEL, D_FF)),
                  _resident((D_FF, D_MODEL)), _resident((1, D_MODEL)), _resident((1, D_MODEL))],
        out_specs=row,
        out_shape=jax.ShapeDtypeStruct((t, D_MODEL), F32),
        scratch_shapes=[pltpu.VMEM((tm, D_FF), BF16)],
        compiler_params=_params("parallel"),
        name="ffn_ln",
    )(x, wg, wu, wd, g, b)


def _in_proj_kernel(x_ref, w_ref, hg_ref, qt_ref, k_ref, vt_ref):
    xb = x_ref[...].astype(BF16)
    u = _dot(xb, w_ref[:, 0:COL_U])
    hg_ref[...] = u[:, :CONV_WIDTH] * jax.nn.sigmoid(u[:, CONV_WIDTH:])
    o = COL_U
    qt_ref[0] = (_dot(xb, w_ref[:, o:o + COL_Q]) * (ATTN_SCALE * LOG2E)).T.astype(BF16)
    o += COL_Q
    k_ref[...] = _dot(xb, w_ref[:, o:o + COL_K]).astype(BF16)
    o += COL_K
    vt_ref[0] = _dot(xb, w_ref[:, o:o + COL_V]).T.astype(BF16)


def _in_proj(x1, w_uqkv, batch, seq):
    t = x1.shape[0]
    tm = min(TOKEN_TILE, seq)
    per_seq = seq // tm
    ncol = COL_U + COL_Q + COL_K + COL_V
    return pl.pallas_call(
        _in_proj_kernel,
        grid=(t // tm,),
        in_specs=[pl.BlockSpec((tm, D_MODEL), lambda i: (i, 0)), _resident((D_MODEL, ncol))],
        out_specs=[pl.BlockSpec((tm, CONV_WIDTH), lambda i: (i, 0)),
                   pl.BlockSpec((1, ATTN_WIDTH, tm), lambda i: (i // per_seq, 0, i % per_seq)),
                   pl.BlockSpec((tm, ATTN_WIDTH), lambda i: (i, 0)),
                   pl.BlockSpec((1, COL_V, tm), lambda i: (i // per_seq, 0, i % per_seq))],
        out_shape=[jax.ShapeDtypeStruct((t, CONV_WIDTH), F32),
                   jax.ShapeDtypeStruct((batch, ATTN_WIDTH, seq), BF16),
                   jax.ShapeDtypeStruct((t, ATTN_WIDTH), BF16),
                   jax.ShapeDtypeStruct((batch, COL_V, seq), BF16)],
        compiler_params=_params("parallel"),
        name="in_proj",
    )(x1, w_uqkv)


def _conv_kernel(prev_ref, cur_ref, next_ref, w_ref, b_ref, g_ref, beta_ref, o_ref, ext_ref, *, tc):
    i = pl.program_id(1)
    last = pl.num_programs(1) - 1
    ext_ref[0:CONV_HALO, :] = jnp.where(i > 0, prev_ref[0], 0.0)
    ext_ref[CONV_HALO:CONV_HALO + tc, :] = cur_ref[0]
    ext_ref[CONV_HALO + tc:2 * CONV_HALO + tc, :] = jnp.where(i < last, next_ref[0], 0.0)
    first_tap = CONV_HALO - CONV_KERNEL // 2
    for r in range(tc // CONV_ROWS):
        r0 = r * CONV_ROWS
        halves = []
        for c in range(CONV_WIDTH // CONV_LANES):
            cs = slice(c * CONV_LANES, (c + 1) * CONV_LANES)
            acc = None
            for b in range(SUBLANES):
                u = None
                for j in range(CONV_KERNEL):
                    if (first_tap + j) % SUBLANES != b:
                        continue
                    e0 = r0 + first_tap + j - b
                    term = w_ref[j:j + 1, cs] * ext_ref[e0:e0 + CONV_ROWS + SUBLANES, cs]
                    u = term if u is None else u + term
                u = u[b:b + CONV_ROWS]
                acc = u if acc is None else acc + u
            halves.append(acc)
        y = _layer_norm(jnp.concatenate(halves, axis=1) + b_ref[...], g_ref[...], beta_ref[...])
        o_ref[0, r0:r0 + CONV_ROWS, :] = (y * jax.nn.sigmoid(y)).astype(BF16)


def _conv_branch(hg, w_dw, b_dw, ln_g, ln_b, batch, seq):
    tc = min(CONV_TILE, seq)
    hpb = tc // CONV_HALO
    nhalo = seq // CONV_HALO
    hg3 = hg.reshape(batch, seq, CONV_WIDTH)
    out = pl.pallas_call(
        functools.partial(_conv_kernel, tc=tc),
        grid=(batch, seq // tc),
        in_specs=[
            pl.BlockSpec((1, CONV_HALO, CONV_WIDTH), lambda b, i: (b, jnp.maximum(i * hpb - 1, 0), 0)),
            pl.BlockSpec((1, tc, CONV_WIDTH), lambda b, i: (b, i, 0)),
            pl.BlockSpec((1, CONV_HALO, CONV_WIDTH),
                         lambda b, i: (b, jnp.minimum((i + 1) * hpb, nhalo - 1), 0)),
            _resident((CONV_KERNEL, CONV_WIDTH)), _resident((1, CONV_WIDTH)),
            _resident((1, CONV_WIDTH)), _resident((1, CONV_WIDTH))],
        out_specs=pl.BlockSpec((1, tc, CONV_WIDTH), lambda b, i: (b, i, 0)),
        out_shape=jax.ShapeDtypeStruct((batch, seq, CONV_WIDTH), BF16),
        scratch_shapes=[pltpu.VMEM((tc + 2 * CONV_HALO, CONV_WIDTH), F32)],
        compiler_params=_params("parallel", "parallel"),
        name="conv_branch",
    )(hg3, hg3, hg3, w_dw, b_dw, ln_g, ln_b)
    return out.reshape(batch * seq, CONV_WIDTH)


def _t5_bucket(rel):
    nb = NUM_BUCKETS // 2
    ret = jnp.where(rel > 0, nb, 0)
    n = jnp.abs(rel)
    max_exact = nb // 2
    nf = jnp.maximum(n, 1).astype(F32)
    large = max_exact + (jnp.log(nf / max_exact) / math.log(MAX_DISTANCE / max_exact)
                         * (nb - max_exact)).astype(jnp.int32)
    large = jnp.minimum(large, nb - 1)
    return ret + jnp.where(n < max_exact, n, large)


def _bias_kernel(tab_ref, o_ref, *, tile):
    h = pl.program_id(0)
    d = pl.program_id(1)
    nb = tile // BIAS_BLOCK
    key = lax.broadcasted_iota(jnp.int32, (BIAS_BLOCK, BIAS_BLOCK), 0)
    query = lax.broadcasted_iota(jnp.int32, (BIAS_BLOCK, BIAS_BLOCK), 1)

    def banded_block(o):
        bucket = _t5_bucket(key - query + o * BIAS_BLOCK)
        acc = jnp.zeros((BIAS_BLOCK, BIAS_BLOCK), F32)
        for b in range(NUM_BUCKETS):
            acc = jnp.where(bucket == b, tab_ref[b, h], acc)
        return acc * LOG2E

    below, diag, above = banded_block(-1), banded_block(0), banded_block(1)
    far_below = tab_ref[NUM_BUCKETS // 2 - 1, h] * LOG2E
    far_above = tab_ref[NUM_BUCKETS - 1, h] * LOG2E
    for kb in range(nb):
        for qb in range(nb):
            o = kb - qb + nb * (d - 1)
            block = jnp.where(o == -1, below, jnp.where(o == 0, diag, above))
            block = jnp.where(o <= -2, far_below, jnp.where(o >= 2, far_above, block))
            o_ref[0, 0, kb * BIAS_BLOCK:(kb + 1) * BIAS_BLOCK, qb * BIAS_BLOCK:(qb + 1) * BIAS_BLOCK] = block


def _bias_tiles(table, tile):
    return pl.pallas_call(
        functools.partial(_bias_kernel, tile=tile),
        grid=(N_HEADS, 3),
        in_specs=[pl.BlockSpec(memory_space=pltpu.SMEM)],
        out_specs=pl.BlockSpec((1, 1, tile, tile), lambda h, d: (h, d, 0, 0)),
        out_shape=jax.ShapeDtypeStruct((N_HEADS, 3, tile, tile), F32),
        compiler_params=_params("parallel", "parallel"),
        name="bias_tiles",
    )(table)


def _attn_kernel(tab_ref, qt_ref, k_ref, vt_ref, bias_ref, lq1_ref, lk1_ref, lq2_ref, lk2_ref,
                 o_ref, acc_ref, m_ref, s_ref, drift_ref, *, tile, nk):
    h = pl.program_id(1)
    qi = pl.program_id(2)
    qt = qt_ref[0]
    row = lax.broadcasted_iota(jnp.int32, qt.shape, 0)
    zero = jnp.zeros_like(qt)
    qt_maps = (jnp.where(row < HEAD_DIM, qt, zero), jnp.where(row >= HEAD_DIM, qt, zero))
    ones = jnp.ones((ONES_ROWS, tile), BF16)

    def keys(ki):
        k0 = pl.multiple_of(ki * tile, tile)
        vt_aug = jnp.concatenate([vt_ref[0, :, pl.ds(k0, tile)], ones], axis=0)
        return k_ref[pl.ds(k0, tile), :], vt_aug

    def logits(k, mi, bias):
        s = _dot(k, qt_maps[mi])
        return s if bias is None else s + bias

    def exact_update(stage, mi):
        ki, bias, shift = stage
        k, vt_aug = keys(ki)
        s_ref[...] = logits(k, mi, bias)
        m_old = m_ref[mi]
        m_new = jnp.maximum(m_old, jnp.max(s_ref[...], axis=0, keepdims=True) + shift)
        p = jnp.exp2(s_ref[...] - (m_new - shift)).astype(BF16)
        acc_ref[mi] = jnp.exp2(m_old - m_new) * acc_ref[mi] + _dot(vt_aug, p)
        m_ref[mi] = m_new

    def streamed_update(stage, mi, first=False):
        ki, bias, shift = stage
        k, vt_aug = keys(ki)
        s = logits(k, mi, bias)
        m_old = jnp.max(s[:PROBE_KEYS], axis=0, keepdims=True) + shift if first else m_ref[mi]
        p = jnp.exp2(s - (m_old - shift)).astype(BF16)
        t_max = jnp.max(s, axis=0, keepdims=True) + shift
        m_new = jnp.maximum(m_old, t_max)
        acc_ref[mi] = jnp.exp2(m_old - m_new) * (acc_ref[mi] + _dot(vt_aug, p))
        m_ref[mi] = m_new
        drift_ref[mi] = jnp.maximum(drift_ref[mi], t_max - m_old)

    far_shift = lambda ki: LOG2E * jnp.where(ki < qi, tab_ref[NUM_BUCKETS // 2 - 1, h],
                                             tab_ref[NUM_BUCKETS - 1, h])

    lo = jnp.maximum(qi - 1, 0)
    hi = jnp.minimum(qi + 2, nk)

    def near_stage(ki):
        return ki, bias_ref[0, ki - qi + 1], 0.0

    def far_stage(j):
        ki = jnp.where(j < lo, j, j + (hi - lo))
        return ki, None, far_shift(ki)

    def reset():
        m_ref[...] = jnp.full(m_ref.shape, -jnp.inf, F32)
        acc_ref[...] = jnp.zeros(acc_ref.shape, F32)

    def sweep(update, stage_of, start, stop):
        def body(i, carry):
            stage = stage_of(i)
            for mi in range(2):
                update(stage, mi)
            return carry

        lax.fori_loop(start, stop, body, 0)

    n_far = nk - (hi - lo)
    reset()
    drift_ref[...] = jnp.zeros(drift_ref.shape, F32)
    sweep(functools.partial(streamed_update, first=True), near_stage, lo, lo + 1)
    sweep(streamed_update, near_stage, lo + 1, hi)
    sweep(streamed_update, far_stage, 0, n_far)

    @pl.when(jnp.max(drift_ref[...]) > MAX_EXPONENT_DRIFT)
    def _():
        reset()
        sweep(exact_update, near_stage, lo, hi)
        sweep(exact_update, far_stage, 0, n_far)

    lam = (jnp.exp(jnp.sum(lq1_ref[...] * lk1_ref[...], keepdims=True))
           - jnp.exp(jnp.sum(lq2_ref[...] * lk2_ref[...], keepdims=True)) + LAMBDA_INIT)
    o1 = acc_ref[0]
    o2 = acc_ref[1]
    o_ref[0] = o1[:V_DIM] / o1[V_DIM:V_DIM + 1] - lam * (o2[:V_DIM] / o2[V_DIM:V_DIM + 1])


def _attention(qt, k, vt, table, bias, lq1, lk1, lq2, lk2, batch, seq, tile):
    nk = seq // tile
    assert seq % tile == 0 and tile % BIAS_BLOCK == 0 and BIAS_BLOCK >= MAX_DISTANCE
    small = _resident((1, HEAD_DIM))
    return pl.pallas_call(
        functools.partial(_attn_kernel, tile=tile, nk=nk),
        grid=(batch, N_HEADS, nk),
        in_specs=[
            pl.BlockSpec(memory_space=pltpu.SMEM),
            pl.BlockSpec((1, V_DIM, tile), lambda b, h, i: (b, h, i)),
            pl.BlockSpec((seq, V_DIM), lambda b, h, i: (b, h)),
            pl.BlockSpec((1, V_DIM, seq), lambda b, h, i: (b, h, 0)),
            pl.BlockSpec((1, 3, tile, tile), lambda b, h, i: (h, 0, 0, 0), pipeline_mode=pl.Buffered(1)),
            small, small, small, small],
        out_specs=pl.BlockSpec((1, V_DIM, tile), lambda b, h, i: (b, h, i)),
        out_shape=jax.ShapeDtypeStruct((batch, COL_V, seq), F32),
        scratch_shapes=[pltpu.VMEM((2, V_DIM + ONES_ROWS, tile), F32), pltpu.VMEM((2, 1, tile), F32),
                        pltpu.VMEM((tile, tile), F32), pltpu.VMEM((2, 1, tile), F32)],
        compiler_params=_params("parallel", "parallel", "parallel"),
        name="diff_attention",
    )(table, qt, k, vt, bias, lq1, lk1, lq2, lk2)


def _merge_ln_kernel(x_ref, hc_ref, at_ref, sg_ref, wco_ref, wao_ref, wg_ref, bg_ref, wo_ref, g_ref, b_ref,
                     o_ref):
    x = x_ref[...]
    xb = x.astype(BF16)
    y_conv = _dot(hc_ref[...], wco_ref[...])
    heads = []
    for hd in range(N_HEADS):
        a = at_ref[0, hd * V_DIM:(hd + 1) * V_DIM, :]
        r = a * lax.rsqrt(jnp.mean(a * a, axis=0, keepdims=True) + LN_EPS) * sg_ref[...]
        heads.append(r * (1.0 - LAMBDA_INIT))
    att = jnp.concatenate(heads, axis=0).T.astype(BF16)
    y_attn = _dot(att, wao_ref[...])
    g_conv = jax.nn.sigmoid(_dot(xb, wg_ref[:, :D_MODEL]) + bg_ref[:, :D_MODEL])
    g_attn = jax.nn.sigmoid(_dot(xb, wg_ref[:, D_MODEL:]) + bg_ref[:, D_MODEL:])
    merged = g_conv * y_conv + g_attn * y_attn
    m = _dot(merged.astype(BF16), wo_ref[...])
    o_ref[...] = _layer_norm(ALPHA * x + m, g_ref[...], b_ref[...])


def _merge_ln(x1, hc, att_t, subln_g, wco, wao, wg, bg, wo, g, b, seq):
    t = x1.shape[0]
    tm = min(TOKEN_TILE, seq)
    per_seq = seq // tm
    row = pl.BlockSpec((tm, D_MODEL), lambda i: (i, 0))
    half = pl.BlockSpec((tm, CONV_WIDTH), lambda i: (i, 0))
    return pl.pallas_call(
        _merge_ln_kernel,
        grid=(t // tm,),
        in_specs=[row, half, pl.BlockSpec((1, COL_V, tm), lambda i: (i // per_seq, 0, i % per_seq)),
                  _resident((V_DIM, 1)), _resident((CONV_WIDTH, D_MODEL)), _resident((ATTN_WIDTH, D_MODEL)),
                  _resident((D_MODEL, 2 * D_MODEL)), _resident((1, 2 * D_MODEL)),
                  _resident((D_MODEL, D_MODEL)), _resident((1, D_MODEL)), _resident((1, D_MODEL))],
        out_specs=row,
        out_shape=jax.ShapeDtypeStruct((t, D_MODEL), F32),
        compiler_params=_params("parallel"),
        name="merge_ln",
    )(x1, hc, att_t, subln_g, wco, wao, wg, bg, wo, g, b)


def _layer(x, p, bias, tile):
    batch, seq, _ = x.shape
    x0 = x.reshape(batch * seq, D_MODEL)
    x1 = _ffn_ln(x0, p["ffn1_wg"], p["ffn1_wu"], p["ffn1_wd"], p["ln1_g"], p["ln1_b"])
    hg, qt, k, vt = _in_proj(x1, p["w_uqkv"], batch, seq)
    hc = _conv_branch(hg, p["conv_w"], p["conv_b"], p["conv_ln_g"], p["conv_ln_b"], batch, seq)
    att_t = _attention(qt, k, vt, p["table"], bias, p["lq1"], p["lk1"], p["lq2"], p["lk2"], batch, seq, tile)
    x2 = _merge_ln(x1, hc, att_t, p["subln_g"], p["w_conv_out"], p["w_attn_out"], p["w_gate"], p["b_gate"],
                   p["w_o"], p["ln2_g"], p["ln2_b"], seq)
    x3 = _ffn_ln(x2, p["ffn2_wg"], p["ffn2_wu"], p["ffn2_wd"], p["ln3_g"], p["ln3_b"])
    return x3.reshape(batch, seq, D_MODEL)


def _prepare(rel_bias_table, ffn1_w_gu, ffn1_w_down, ln1_g, ln1_b, w_in, b_gate, conv_w_dw, conv_b_dw,
             conv_ln_g, conv_ln_b, w_conv_out, lambda_q1, lambda_k1, lambda_q2, lambda_k2, subln_g,
             w_attn_out, w_o, ln2_g, ln2_b, ffn2_w_gu, ffn2_w_down, ln3_g, ln3_b):
    l = 0
    n_uqkv = COL_U + COL_Q + COL_K + COL_V
    row = lambda a: a[l].reshape(1, -1).astype(F32)
    return {
        "table": rel_bias_table.astype(F32),
        "ffn1_wg": ffn1_w_gu[l, :, :D_FF].astype(BF16), "ffn1_wu": ffn1_w_gu[l, :, D_FF:].astype(BF16),
        "ffn1_wd": ffn1_w_down[l].astype(BF16), "ln1_g": row(ln1_g), "ln1_b": row(ln1_b),
        "w_uqkv": w_in[l, :, :n_uqkv].astype(BF16), "w_gate": w_in[l, :, n_uqkv:].astype(BF16),
        "b_gate": row(b_gate),
        "conv_w": conv_w_dw[l].reshape(CONV_KERNEL, CONV_WIDTH).astype(F32), "conv_b": row(conv_b_dw),
        "conv_ln_g": row(conv_ln_g), "conv_ln_b": row(conv_ln_b),
        "w_conv_out": w_conv_out[l].astype(BF16),
        "lq1": row(lambda_q1), "lk1": row(lambda_k1), "lq2": row(lambda_q2), "lk2": row(lambda_k2),
        "subln_g": subln_g[l].reshape(-1, 1).astype(F32), "w_attn_out": w_attn_out[l].astype(BF16), "w_o": w_o[l].astype(BF16),
        "ln2_g": row(ln2_g), "ln2_b": row(ln2_b),
        "ffn2_wg": ffn2_w_gu[l, :, :D_FF].astype(BF16), "ffn2_wu": ffn2_w_gu[l, :, D_FF:].astype(BF16),
        "ffn2_wd": ffn2_w_down[l].astype(BF16), "ln3_g": row(ln3_g), "ln3_b": row(ln3_b),
    }


def kernel(x_prompt, x_sample, rel_bias_table, ffn1_w_gu, ffn1_w_down, ln1_g, ln1_b, w_in, b_gate, conv_w_dw, conv_b_dw, conv_ln_g, conv_ln_b, w_conv_out, lambda_q1, lambda_k1, lambda_q2, lambda_k2, subln_g, w_attn_out, w_o, ln2_g, ln2_b, ffn2_w_gu, ffn2_w_down, ln3_g, ln3_b):
    p = _prepare(rel_bias_table, ffn1_w_gu, ffn1_w_down, ln1_g, ln1_b, w_in, b_gate, conv_w_dw, conv_b_dw,
                 conv_ln_g, conv_ln_b, w_conv_out, lambda_q1, lambda_k1, lambda_q2, lambda_k2, subln_g,
                 w_attn_out, w_o, ln2_g, ln2_b, ffn2_w_gu, ffn2_w_down, ln3_g, ln3_b)
    tile = min(ATTN_TILE, x_prompt.shape[1], x_sample.shape[1])
    bias = _bias_tiles(p["table"], tile)
    return (_layer(x_prompt, p, bias, tile), _layer(x_sample, p, bias, tile))
```

```python
import functools
import math

import jax
import jax.numpy as jnp
from jax import lax
from jax.experimental import pallas as pl
from jax.experimental.pallas import tpu as pltpu

F32 = jnp.float32
BF16 = jnp.bfloat16

D_MODEL = 1024
N_HEADS = 4
HEAD_DIM = 64
V_DIM = 2 * HEAD_DIM
ATTN_WIDTH = N_HEADS * 2 * HEAD_DIM
CONV_WIDTH = 512
CONV_KERNEL = 31
CONV_HALO = 16
D_FF = 2816
NUM_BUCKETS = 32
MAX_DISTANCE = 128
DEPTH = 1
ALPHA = (2.0 * DEPTH) ** 0.25
LN_EPS = 1e-5
ATTN_SCALE = HEAD_DIM ** -0.5
LAMBDA_INIT = 0.8 - 0.6 * math.exp(-0.3 * 0)
LOG2E = math.log2(math.e)
ONES_ROWS = 16
COL_U = 2 * CONV_WIDTH
COL_Q = ATTN_WIDTH
COL_K = ATTN_WIDTH
COL_V = N_HEADS * V_DIM

V7X_VMEM_LIMIT_BYTES = 56 * 1024 * 1024
FF_CHUNK = 256
TOKEN_TILE = 1024
CONV_TILE = 256
CONV_ROWS = 64
CONV_LANES = 256
SUBLANES = 8
ATTN_TILE = 1024
FAR_TILES_PER_TRIP = 2
MAX_EXPONENT_DRIFT = 64.0
PROBE_KEYS = 16
BIAS_BLOCK = 128


def _dot(a, b):
    return jnp.dot(a, b, preferred_element_type=F32)


def _layer_norm(z, g, b):
    mu = jnp.mean(z, axis=-1, keepdims=True)
    zc = z - mu
    var = jnp.mean(zc * zc, axis=-1, keepdims=True)
    return zc * lax.rsqrt(var + LN_EPS) * g + b


def _resident(shape):
    return pl.BlockSpec(shape, lambda *_: (0,) * len(shape), pipeline_mode=pl.Buffered(1))


def _params(*semantics):
    return pltpu.CompilerParams(dimension_semantics=semantics,
                                vmem_limit_bytes=V7X_VMEM_LIMIT_BYTES)


def _ffn_ln_kernel(x_ref, wg_ref, wu_ref, wd_ref, g_ref, b_ref, o_ref, h_ref):
    x = x_ref[...]
    xb = x.astype(BF16)
    for c in range(D_FF // FF_CHUNK):
        sl = slice(c * FF_CHUNK, (c + 1) * FF_CHUNK)
        a = _dot(xb, wg_ref[:, sl])
        u = _dot(xb, wu_ref[:, sl])
        h_ref[:, sl] = (a * jax.nn.sigmoid(a) * u).astype(BF16)
    y = _dot(h_ref[...], wd_ref[...])
    o_ref[...] = _layer_norm(ALPHA * x + 0.5 * y, g_ref[...], b_ref[...])


def _ffn_ln(x, wg, wu, wd, g, b):
    t = x.shape[0]
    tm = min(TOKEN_TILE, t)
    row = pl.BlockSpec((tm, D_MODEL), lambda i: (i, 0))
    return pl.pallas_call(
        _ffn_ln_kernel,
        grid=(t // tm,),
        in_specs=[row, _resident((D_MODEL, D_FF)), _resident((D_MODEL, D_FF)),
                  _resident((D_FF, D_MODEL)), _resident((1, D_MODEL)), _resident((1, D_MODEL))],
        out_specs=row,
        out_shape=jax.ShapeDtypeStruct((t, D_MODEL), F32),
        scratch_shapes=[pltpu.VMEM((tm, D_FF), BF16)],
        compiler_params=_params("parallel"),
        name="ffn_ln",
    )(x, wg, wu, wd, g, b)


def _in_proj_kernel(x_ref, w_ref, hg_ref, qt_ref, k_ref, vt_ref):
    xb = x_ref[...].astype(BF16)
    u = _dot(xb, w_ref[:, 0:COL_U])
    hg_ref[...] = u[:, :CONV_WIDTH] * jax.nn.sigmoid(u[:, CONV_WIDTH:])
    o = COL_U
    qt_ref[0] = (_dot(xb, w_ref[:, o:o + COL_Q]) * (ATTN_SCALE * LOG2E)).T.astype(BF16)
    o += COL_Q
    k_ref[...] = _dot(xb, w_ref[:, o:o + COL_K]).astype(BF16)
    o += COL_K
    vt_ref[0] = _dot(xb, w_ref[:, o:o + COL_V]).T.astype(BF16)


def _in_proj(x1, w_uqkv, batch, seq):
    t = x1.shape[0]
    tm = min(TOKEN_TILE, seq)
    per_seq = seq // tm
    ncol = COL_U + COL_Q + COL_K + COL_V
    return pl.pallas_call(
        _in_proj_kernel,
        grid=(t // tm,),
        in_specs=[pl.BlockSpec((tm, D_MODEL), lambda i: (i, 0)), _resident((D_MODEL, ncol))],
        out_specs=[pl.BlockSpec((tm, CONV_WIDTH), lambda i: (i, 0)),
                   pl.BlockSpec((1, ATTN_WIDTH, tm), lambda i: (i // per_seq, 0, i % per_seq)),
                   pl.BlockSpec((tm, ATTN_WIDTH), lambda i: (i, 0)),
                   pl.BlockSpec((1, COL_V, tm), lambda i: (i // per_seq, 0, i % per_seq))],
        out_shape=[jax.ShapeDtypeStruct((t, CONV_WIDTH), F32),
                   jax.ShapeDtypeStruct((batch, ATTN_WIDTH, seq), BF16),
                   jax.ShapeDtypeStruct((t, ATTN_WIDTH), BF16),
                   jax.ShapeDtypeStruct((batch, COL_V, seq), BF16)],
        compiler_params=_params("parallel"),
        name="in_proj",
    )(x1, w_uqkv)


def _conv_kernel(prev_ref, cur_ref, next_ref, w_ref, b_ref, g_ref, beta_ref, o_ref, ext_ref, *, tc):
    i = pl.program_id(1)
    last = pl.num_programs(1) - 1
    ext_ref[0:CONV_HALO, :] = jnp.where(i > 0, prev_ref[0], 0.0)
    ext_ref[CONV_HALO:CONV_HALO + tc, :] = cur_ref[0]
    ext_ref[CONV_HALO + tc:2 * CONV_HALO + tc, :] = jnp.where(i < last, next_ref[0], 0.0)
    first_tap = CONV_HALO - CONV_KERNEL // 2
    for r in range(tc // CONV_ROWS):
        r0 = r * CONV_ROWS
        halves = []
        for c in range(CONV_WIDTH // CONV_LANES):
            cs = slice(c * CONV_LANES, (c + 1) * CONV_LANES)
            acc = None
            for b in range(SUBLANES):
                u = None
                for j in range(CONV_KERNEL):
                    if (first_tap + j) % SUBLANES != b:
                        continue
                    e0 = r0 + first_tap + j - b
                    term = w_ref[j:j + 1, cs] * ext_ref[e0:e0 + CONV_ROWS + SUBLANES, cs]
                    u = term if u is None else u + term
                u = u[b:b + CONV_ROWS]
                acc = u if acc is None else acc + u
            halves.append(acc)
        y = _layer_norm(jnp.concatenate(halves, axis=1) + b_ref[...], g_ref[...], beta_ref[...])
        o_ref[0, r0:r0 + CONV_ROWS, :] = (y * jax.nn.sigmoid(y)).astype(BF16)


def _conv_branch(hg, w_dw, b_dw, ln_g, ln_b, batch, seq):
    tc = min(CONV_TILE, seq)
    hpb = tc // CONV_HALO
    nhalo = seq // CONV_HALO
    hg3 = hg.reshape(batch, seq, CONV_WIDTH)
    out = pl.pallas_call(
        functools.partial(_conv_kernel, tc=tc),
        grid=(batch, seq // tc),
        in_specs=[
            pl.BlockSpec((1, CONV_HALO, CONV_WIDTH), lambda b, i: (b, jnp.maximum(i * hpb - 1, 0), 0)),
            pl.BlockSpec((1, tc, CONV_WIDTH), lambda b, i: (b, i, 0)),
            pl.BlockSpec((1, CONV_HALO, CONV_WIDTH),
                         lambda b, i: (b, jnp.minimum((i + 1) * hpb, nhalo - 1), 0)),
            _resident((CONV_KERNEL, CONV_WIDTH)), _resident((1, CONV_WIDTH)),
            _resident((1, CONV_WIDTH)), _resident((1, CONV_WIDTH))],
        out_specs=pl.BlockSpec((1, tc, CONV_WIDTH), lambda b, i: (b, i, 0)),
        out_shape=jax.ShapeDtypeStruct((batch, seq, CONV_WIDTH), BF16),
        scratch_shapes=[pltpu.VMEM((tc + 2 * CONV_HALO, CONV_WIDTH), F32)],
        compiler_params=_params("parallel", "parallel"),
        name="conv_branch",
    )(hg3, hg3, hg3, w_dw, b_dw, ln_g, ln_b)
    return out.reshape(batch * seq, CONV_WIDTH)


def _t5_bucket(rel):
    nb = NUM_BUCKETS // 2
    ret = jnp.where(rel > 0, nb, 0)
    n = jnp.abs(rel)
    max_exact = nb // 2
    nf = jnp.maximum(n, 1).astype(F32)
    large = max_exact + (jnp.log(nf / max_exact) / math.log(MAX_DISTANCE / max_exact)
                         * (nb - max_exact)).astype(jnp.int32)
    large = jnp.minimum(large, nb - 1)
    return ret + jnp.where(n < max_exact, n, large)


def _bias_kernel(tab_ref, o_ref, *, tile):
    h = pl.program_id(0)
    d = pl.program_id(1)
    nb = tile // BIAS_BLOCK
    key = lax.broadcasted_iota(jnp.int32, (BIAS_BLOCK, BIAS_BLOCK), 0)
    query = lax.broadcasted_iota(jnp.int32, (BIAS_BLOCK, BIAS_BLOCK), 1)

    def banded_block(o):
        bucket = _t5_bucket(key - query + o * BIAS_BLOCK)
        acc = jnp.zeros((BIAS_BLOCK, BIAS_BLOCK), F32)
        for b in range(NUM_BUCKETS):
            acc = jnp.where(bucket == b, tab_ref[b, h], acc)
        return acc * LOG2E

    below, diag, above = banded_block(-1), banded_block(0), banded_block(1)
    far_below = tab_ref[NUM_BUCKETS // 2 - 1, h] * LOG2E
    far_above = tab_ref[NUM_BUCKETS - 1, h] * LOG2E
    for kb in range(nb):
        for qb in range(nb):
            o = kb - qb + nb * (d - 1)
            block = jnp.where(o == -1, below, jnp.where(o == 0, diag, above))
            block = jnp.where(o <= -2, far_below, jnp.where(o >= 2, far_above, block))
            o_ref[0, 0, kb * BIAS_BLOCK:(kb + 1) * BIAS_BLOCK, qb * BIAS_BLOCK:(qb + 1) * BIAS_BLOCK] = block


def _bias_tiles(table, tile):
    return pl.pallas_call(
        functools.partial(_bias_kernel, tile=tile),
        grid=(N_HEADS, 3),
        in_specs=[pl.BlockSpec(memory_space=pltpu.SMEM)],
        out_specs=pl.BlockSpec((1, 1, tile, tile), lambda h, d: (h, d, 0, 0)),
        out_shape=jax.ShapeDtypeStruct((N_HEADS, 3, tile, tile), F32),
        compiler_params=_params("parallel", "parallel"),
        name="bias_tiles",
    )(table)


def _attn_kernel(tab_ref, qt_ref, k_ref, vt_ref, bias_ref, lq1_ref, lk1_ref, lq2_ref, lk2_ref,
                 o_ref, acc_ref, m_ref, s_ref, drift_ref, *, tile, nk):
    h = pl.program_id(1)
    qi = pl.program_id(2)
    qt = qt_ref[0]
    row = lax.broadcasted_iota(jnp.int32, qt.shape, 0)
    zero = jnp.zeros_like(qt)
    qt_maps = (jnp.where(row < HEAD_DIM, qt, zero), jnp.where(row >= HEAD_DIM, qt, zero))
    ones = jnp.ones((ONES_ROWS, tile), BF16)

    def keys(ki):
        k0 = pl.multiple_of(ki * tile, tile)
        vt_aug = jnp.concatenate([vt_ref[0, :, pl.ds(k0, tile)], ones], axis=0)
        return k_ref[pl.ds(k0, tile), :], vt_aug

    def logits(k, mi, bias):
        s = _dot(k, qt_maps[mi])
        return s if bias is None else s + bias

    def exact_update(stage, mi):
        ki, bias, shift = stage
        k, vt_aug = keys(ki)
        s_ref[...] = logits(k, mi, bias)
        m_old = m_ref[mi]
        m_new = jnp.maximum(m_old, jnp.max(s_ref[...], axis=0, keepdims=True) + shift)
        p = jnp.exp2(s_ref[...] - (m_new - shift)).astype(BF16)
        acc_ref[mi] = jnp.exp2(m_old - m_new) * acc_ref[mi] + _dot(vt_aug, p)
        m_ref[mi] = m_new

    def streamed_update(stage, mi, first=False):
        ki, bias, shift = stage
        k, vt_aug = keys(ki)
        s = logits(k, mi, bias)
        m_old = jnp.max(s[:PROBE_KEYS], axis=0, keepdims=True) + shift if first else m_ref[mi]
        p = jnp.exp2(s - (m_old - shift)).astype(BF16)
        t_max = jnp.max(s, axis=0, keepdims=True) + shift
        m_new = jnp.maximum(m_old, t_max)
        acc_ref[mi] = jnp.exp2(m_old - m_new) * (acc_ref[mi] + _dot(vt_aug, p))
        m_ref[mi] = m_new
        drift_ref[mi] = jnp.maximum(drift_ref[mi], t_max - m_old)

    far_shift = lambda ki: LOG2E * jnp.where(ki < qi, tab_ref[NUM_BUCKETS // 2 - 1, h],
                                             tab_ref[NUM_BUCKETS - 1, h])

    lo = jnp.maximum(qi - 1, 0)
    hi = jnp.minimum(qi + 2, nk)

    def near_stage(ki):
        return ki, bias_ref[0, ki - qi + 1], 0.0

    def far_stage(j):
        ki = jnp.where(j < lo, j, j + (hi - lo))
        return ki, None, far_shift(ki)

    def reset():
        m_ref[...] = jnp.full(m_ref.shape, -jnp.inf, F32)
        acc_ref[...] = jnp.zeros(acc_ref.shape, F32)

    def sweep(update, stage_of, start, stop, tiles_per_trip=1):
        def trip(width):
            def body(i, carry):
                for u in range(width):
                    stage = stage_of(start + i * width + u)
                    for mi in range(2):
                        update(stage, mi)
                return carry
            return body

        full = (stop - start) // tiles_per_trip
        lax.fori_loop(0, full, trip(tiles_per_trip), 0)
        if tiles_per_trip > 1:
            done = start + full * tiles_per_trip
            lax.fori_loop(0, stop - done, lambda i, c: trip(1)(done - start + i, c), 0)

    n_far = nk - (hi - lo)
    reset()
    drift_ref[...] = jnp.zeros(drift_ref.shape, F32)
    sweep(functools.partial(streamed_update, first=True), near_stage, lo, lo + 1)
    sweep(streamed_update, near_stage, lo + 1, hi)
    sweep(streamed_update, far_stage, 0, n_far, FAR_TILES_PER_TRIP)

    @pl.when(jnp.max(drift_ref[...]) > MAX_EXPONENT_DRIFT)
    def _():
        reset()
        sweep(exact_update, near_stage, lo, hi)
        sweep(exact_update, far_stage, 0, n_far)

    lam = (jnp.exp(jnp.sum(lq1_ref[...] * lk1_ref[...], keepdims=True))
           - jnp.exp(jnp.sum(lq2_ref[...] * lk2_ref[...], keepdims=True)) + LAMBDA_INIT)
    o1 = acc_ref[0]
    o2 = acc_ref[1]
    o_ref[0] = o1[:V_DIM] / o1[V_DIM:V_DIM + 1] - lam * (o2[:V_DIM] / o2[V_DIM:V_DIM + 1])


def _attention(qt, k, vt, table, bias, lq1, lk1, lq2, lk2, batch, seq, tile):
    nk = seq // tile
    assert seq % tile == 0 and tile % BIAS_BLOCK == 0 and BIAS_BLOCK >= MAX_DISTANCE
    small = _resident((1, HEAD_DIM))
    return pl.pallas_call(
        functools.partial(_attn_kernel, tile=tile, nk=nk),
        grid=(batch, N_HEADS, nk),
        in_specs=[
            pl.BlockSpec(memory_space=pltpu.SMEM),
            pl.BlockSpec((1, V_DIM, tile), lambda b, h, i: (b, h, i)),
            pl.BlockSpec((seq, V_DIM), lambda b, h, i: (b, h), pipeline_mode=pl.Buffered(1)),
            pl.BlockSpec((1, V_DIM, seq), lambda b, h, i: (b, h, 0), pipeline_mode=pl.Buffered(1)),
            pl.BlockSpec((1, 3, tile, tile), lambda b, h, i: (h, 0, 0, 0), pipeline_mode=pl.Buffered(1)),
            small, small, small, small],
        out_specs=pl.BlockSpec((1, V_DIM, tile), lambda b, h, i: (b, h, i)),
        out_shape=jax.ShapeDtypeStruct((batch, COL_V, seq), F32),
        scratch_shapes=[pltpu.VMEM((2, V_DIM + ONES_ROWS, tile), F32), pltpu.VMEM((2, 1, tile), F32),
                        pltpu.VMEM((tile, tile), F32), pltpu.VMEM((2, 1, tile), F32)],
        compiler_params=_params("parallel", "parallel", "parallel"),
        name="diff_attention",
    )(table, qt, k, vt, bias, lq1, lk1, lq2, lk2)


def _merge_ln_kernel(x_ref, hc_ref, at_ref, sg_ref, wco_ref, wao_ref, wg_ref, bg_ref, wo_ref, g_ref, b_ref,
                     o_ref):
    x = x_ref[...]
    xb = x.astype(BF16)
    y_conv = _dot(hc_ref[...], wco_ref[...])
    heads = []
    for hd in range(N_HEADS):
        a = at_ref[0, hd * V_DIM:(hd + 1) * V_DIM, :]
        r = a * lax.rsqrt(jnp.mean(a * a, axis=0, keepdims=True) + LN_EPS) * sg_ref[...]
        heads.append(r * (1.0 - LAMBDA_INIT))
    att = jnp.concatenate(heads, axis=0).T.astype(BF16)
    y_attn = _dot(att, wao_ref[...])
    g_conv = jax.nn.sigmoid(_dot(xb, wg_ref[:, :D_MODEL]) + bg_ref[:, :D_MODEL])
    g_attn = jax.nn.sigmoid(_dot(xb, wg_ref[:, D_MODEL:]) + bg_ref[:, D_MODEL:])
    merged = g_conv * y_conv + g_attn * y_attn
    m = _dot(merged.astype(BF16), wo_ref[...])
    o_ref[...] = _layer_norm(ALPHA * x + m, g_ref[...], b_ref[...])


def _merge_ln(x1, hc, att_t, subln_g, wco, wao, wg, bg, wo, g, b, seq):
    t = x1.shape[0]
    tm = min(TOKEN_TILE, seq)
    per_seq = seq // tm
    row = pl.BlockSpec((tm, D_MODEL), lambda i: (i, 0))
    half = pl.BlockSpec((tm, CONV_WIDTH), lambda i: (i, 0))
    return pl.pallas_call(
        _merge_ln_kernel,
        grid=(t // tm,),
        in_specs=[row, half, pl.BlockSpec((1, COL_V, tm), lambda i: (i // per_seq, 0, i % per_seq)),
                  _resident((V_DIM, 1)), _resident((CONV_WIDTH, D_MODEL)), _resident((ATTN_WIDTH, D_MODEL)),
                  _resident((D_MODEL, 2 * D_MODEL)), _resident((1, 2 * D_MODEL)),
                  _resident((D_MODEL, D_MODEL)), _resident((1, D_MODEL)), _resident((1, D_MODEL))],
        out_specs=row,
        out_shape=jax.ShapeDtypeStruct((t, D_MODEL), F32),
        compiler_params=_params("parallel"),
        name="merge_ln",
    )(x1, hc, att_t, subln_g, wco, wao, wg, bg, wo, g, b)


def _layer(x, p, bias, tile):
    batch, seq, _ = x.shape
    x0 = x.reshape(batch * seq, D_MODEL)
    x1 = _ffn_ln(x0, p["ffn1_wg"], p["ffn1_wu"], p["ffn1_wd"], p["ln1_g"], p["ln1_b"])
    hg, qt, k, vt = _in_proj(x1, p["w_uqkv"], batch, seq)
    hc = _conv_branch(hg, p["conv_w"], p["conv_b"], p["conv_ln_g"], p["conv_ln_b"], batch, seq)
    att_t = _attention(qt, k, vt, p["table"], bias, p["lq1"], p["lk1"], p["lq2"], p["lk2"], batch, seq, tile)
    x2 = _merge_ln(x1, hc, att_t, p["subln_g"], p["w_conv_out"], p["w_attn_out"], p["w_gate"], p["b_gate"],
                   p["w_o"], p["ln2_g"], p["ln2_b"], seq)
    x3 = _ffn_ln(x2, p["ffn2_wg"], p["ffn2_wu"], p["ffn2_wd"], p["ln3_g"], p["ln3_b"])
    return x3.reshape(batch, seq, D_MODEL)


def _prepare(rel_bias_table, ffn1_w_gu, ffn1_w_down, ln1_g, ln1_b, w_in, b_gate, conv_w_dw, conv_b_dw,
             conv_ln_g, conv_ln_b, w_conv_out, lambda_q1, lambda_k1, lambda_q2, lambda_k2, subln_g,
             w_attn_out, w_o, ln2_g, ln2_b, ffn2_w_gu, ffn2_w_down, ln3_g, ln3_b):
    l = 0
    n_uqkv = COL_U + COL_Q + COL_K + COL_V
    row = lambda a: a[l].reshape(1, -1).astype(F32)
    return {
        "table": rel_bias_table.astype(F32),
        "ffn1_wg": ffn1_w_gu[l, :, :D_FF].astype(BF16), "ffn1_wu": ffn1_w_gu[l, :, D_FF:].astype(BF16),
        "ffn1_wd": ffn1_w_down[l].astype(BF16), "ln1_g": row(ln1_g), "ln1_b": row(ln1_b),
        "w_uqkv": w_in[l, :, :n_uqkv].astype(BF16), "w_gate": w_in[l, :, n_uqkv:].astype(BF16),
        "b_gate": row(b_gate),
        "conv_w": conv_w_dw[l].reshape(CONV_KERNEL, CONV_WIDTH).astype(F32), "conv_b": row(conv_b_dw),
        "conv_ln_g": row(conv_ln_g), "conv_ln_b": row(conv_ln_b),
        "w_conv_out": w_conv_out[l].astype(BF16),
        "lq1": row(lambda_q1), "lk1": row(lambda_k1), "lq2": row(lambda_q2), "lk2": row(lambda_k2),
        "subln_g": subln_g[l].reshape(-1, 1).astype(F32), "w_attn_out": w_attn_out[l].astype(BF16), "w_o": w_o[l].astype(BF16),
        "ln2_g": row(ln2_g), "ln2_b": row(ln2_b),
        "ffn2_wg": ffn2_w_gu[l, :, :D_FF].astype(BF16), "ffn2_wu": ffn2_w_gu[l, :, D_FF:].astype(BF16),
        "ffn2_wd": ffn2_w_down[l].astype(BF16), "ln3_g": row(ln3_g), "ln3_b": row(ln3_b),
    }


def kernel(x_prompt, x_sample, rel_bias_table, ffn1_w_gu, ffn1_w_down, ln1_g, ln1_b, w_in, b_gate, conv_w_dw, conv_b_dw, conv_ln_g, conv_ln_b, w_conv_out, lambda_q1, lambda_k1, lambda_q2, lambda_k2, subln_g, w_attn_out, w_o, ln2_g, ln2_b, ffn2_w_gu, ffn2_w_down, ln3_g, ln3_b):
    p = _prepare(rel_bias_table, ffn1_w_gu, ffn1_w_down, ln1_g, ln1_b, w_in, b_gate, conv_w_dw, conv_b_dw,
                 conv_ln_g, conv_ln_b, w_conv_out, lambda_q1, lambda_k1, lambda_q2, lambda_k2, subln_g,
                 w_attn_out, w_o, ln2_g, ln2_b, ffn2_w_gu, ffn2_w_down, ln3_g, ln3_b)
    tile = min(ATTN_TILE, x_prompt.shape[1], x_sample.shape[1])
    bias = _bias_tiles(p["table"], tile)
    return (_layer(x_prompt, p, bias, tile), _layer(x_sample, p, bias, tile))
```

```python
import functools
import math

import jax
import jax.numpy as jnp
from jax import lax
from jax.experimental import pallas as pl
from jax.experimental.pallas import tpu as pltpu

F32 = jnp.float32
BF16 = jnp.bfloat16

D_MODEL = 1024
N_HEADS = 4
HEAD_DIM = 64
V_DIM = 2 * HEAD_DIM
ATTN_WIDTH = N_HEADS * 2 * HEAD_DIM
CONV_WIDTH = 512
CONV_KERNEL = 31
CONV_HALO = 16
D_FF = 2816
NUM_BUCKETS = 32
MAX_DISTANCE = 128
DEPTH = 1
ALPHA = (2.0 * DEPTH) ** 0.25
LN_EPS = 1e-5
ATTN_SCALE = HEAD_DIM ** -0.5
LAMBDA_INIT = 0.8 - 0.6 * math.exp(-0.3 * 0)
LOG2E = math.log2(math.e)
ONES_ROWS = 16
COL_U = 2 * CONV_WIDTH
COL_Q = ATTN_WIDTH
COL_K = ATTN_WIDTH
COL_V = N_HEADS * V_DIM

V7X_VMEM_LIMIT_BYTES = 56 * 1024 * 1024
FF_CHUNK = 256
TOKEN_TILE = 1024
CONV_TILE = 256
CONV_ROWS = 64
CONV_LANES = 256
SUBLANES = 8
ATTN_TILE = 1024
FAR_TILES_PER_TRIP = 2
QUERY_TILES_PER_STEP = 8
MAX_EXPONENT_DRIFT = 64.0
PROBE_KEYS = 16
BIAS_BLOCK = 128


def _dot(a, b):
    return jnp.dot(a, b, preferred_element_type=F32)


def _layer_norm(z, g, b):
    mu = jnp.mean(z, axis=-1, keepdims=True)
    zc = z - mu
    var = jnp.mean(zc * zc, axis=-1, keepdims=True)
    return zc * lax.rsqrt(var + LN_EPS) * g + b


def _resident(shape):
    return pl.BlockSpec(shape, lambda *_: (0,) * len(shape), pipeline_mode=pl.Buffered(1))


def _params(*semantics):
    return pltpu.CompilerParams(dimension_semantics=semantics,
                                vmem_limit_bytes=V7X_VMEM_LIMIT_BYTES)


def _ffn_ln_kernel(x_ref, wg_ref, wu_ref, wd_ref, g_ref, b_ref, o_ref, h_ref):
    x = x_ref[...]
    xb = x.astype(BF16)
    for c in range(D_FF // FF_CHUNK):
        sl = slice(c * FF_CHUNK, (c + 1) * FF_CHUNK)
        a = _dot(xb, wg_ref[:, sl])
        u = _dot(xb, wu_ref[:, sl])
        h_ref[:, sl] = (a * jax.nn.sigmoid(a) * u).astype(BF16)
    y = _dot(h_ref[...], wd_ref[...])
    o_ref[...] = _layer_norm(ALPHA * x + 0.5 * y, g_ref[...], b_ref[...])


def _ffn_ln(x, wg, wu, wd, g, b):
    t = x.shape[0]
    tm = min(TOKEN_TILE, t)
    row = pl.BlockSpec((tm, D_MODEL), lambda i: (i, 0))
    return pl.pallas_call(
        _ffn_ln_kernel,
        grid=(t // tm,),
        in_specs=[row, _resident((D_MODEL, D_FF)), _resident((D_MODEL, D_FF)),
                  _resident((D_FF, D_MODEL)), _resident((1, D_MODEL)), _resident((1, D_MODEL))],
        out_specs=row,
        out_shape=jax.ShapeDtypeStruct((t, D_MODEL), F32),
        scratch_shapes=[pltpu.VMEM((tm, D_FF), BF16)],
        compiler_params=_params("parallel"),
        name="ffn_ln",
    )(x, wg, wu, wd, g, b)


def _in_proj_kernel(x_ref, w_ref, hg_ref, qt_ref, k_ref, vt_ref):
    xb = x_ref[...].astype(BF16)
    u = _dot(xb, w_ref[:, 0:COL_U])
    hg_ref[...] = u[:, :CONV_WIDTH] * jax.nn.sigmoid(u[:, CONV_WIDTH:])
    o = COL_U
    qt_ref[0] = (_dot(xb, w_ref[:, o:o + COL_Q]) * (ATTN_SCALE * LOG2E)).T.astype(BF16)
    o += COL_Q
    k_ref[...] = _dot(xb, w_ref[:, o:o + COL_K]).astype(BF16)
    o += COL_K
    vt_ref[0] = _dot(xb, w_ref[:, o:o + COL_V]).T.astype(BF16)


def _in_proj(x1, w_uqkv, batch, seq):
    t = x1.shape[0]
    tm = min(TOKEN_TILE, seq)
    per_seq = seq // tm
    ncol = COL_U + COL_Q + COL_K + COL_V
    return pl.pallas_call(
        _in_proj_kernel,
        grid=(t // tm,),
        in_specs=[pl.BlockSpec((tm, D_MODEL), lambda i: (i, 0)), _resident((D_MODEL, ncol))],
        out_specs=[pl.BlockSpec((tm, CONV_WIDTH), lambda i: (i, 0)),
                   pl.BlockSpec((1, ATTN_WIDTH, tm), lambda i: (i // per_seq, 0, i % per_seq)),
                   pl.BlockSpec((tm, ATTN_WIDTH), lambda i: (i, 0)),
                   pl.BlockSpec((1, COL_V, tm), lambda i: (i // per_seq, 0, i % per_seq))],
        out_shape=[jax.ShapeDtypeStruct((t, CONV_WIDTH), F32),
                   jax.ShapeDtypeStruct((batch, ATTN_WIDTH, seq), BF16),
                   jax.ShapeDtypeStruct((t, ATTN_WIDTH), BF16),
                   jax.ShapeDtypeStruct((batch, COL_V, seq), BF16)],
        compiler_params=_params("parallel"),
        name="in_proj",
    )(x1, w_uqkv)


def _conv_kernel(prev_ref, cur_ref, next_ref, w_ref, b_ref, g_ref, beta_ref, o_ref, ext_ref, *, tc):
    i = pl.program_id(1)
    last = pl.num_programs(1) - 1
    ext_ref[0:CONV_HALO, :] = jnp.where(i > 0, prev_ref[0], 0.0)
    ext_ref[CONV_HALO:CONV_HALO + tc, :] = cur_ref[0]
    ext_ref[CONV_HALO + tc:2 * CONV_HALO + tc, :] = jnp.where(i < last, next_ref[0], 0.0)
    first_tap = CONV_HALO - CONV_KERNEL // 2
    for r in range(tc // CONV_ROWS):
        r0 = r * CONV_ROWS
        halves = []
        for c in range(CONV_WIDTH // CONV_LANES):
            cs = slice(c * CONV_LANES, (c + 1) * CONV_LANES)
            acc = None
            for b in range(SUBLANES):
                u = None
                for j in range(CONV_KERNEL):
                    if (first_tap + j) % SUBLANES != b:
                        continue
                    e0 = r0 + first_tap + j - b
                    rows = ext_ref[e0:e0 + CONV_ROWS + SUBLANES, cs].reshape(-1, SUBLANES, CONV_LANES)
                    term = rows * w_ref[j, :, cs]
                    u = term if u is None else u + term
                u = u.reshape(CONV_ROWS + SUBLANES, CONV_LANES)[b:b + CONV_ROWS]
                acc = u if acc is None else acc + u
            halves.append(acc)
        y = _layer_norm(jnp.concatenate(halves, axis=1) + b_ref[...], g_ref[...], beta_ref[...])
        o_ref[0, r0:r0 + CONV_ROWS, :] = (y * jax.nn.sigmoid(y)).astype(BF16)


def _conv_branch(hg, w_dw, b_dw, ln_g, ln_b, batch, seq):
    tc = min(CONV_TILE, seq)
    hpb = tc // CONV_HALO
    nhalo = seq // CONV_HALO
    hg3 = hg.reshape(batch, seq, CONV_WIDTH)
    out = pl.pallas_call(
        functools.partial(_conv_kernel, tc=tc),
        grid=(batch, seq // tc),
        in_specs=[
            pl.BlockSpec((1, CONV_HALO, CONV_WIDTH), lambda b, i: (b, jnp.maximum(i * hpb - 1, 0), 0)),
            pl.BlockSpec((1, tc, CONV_WIDTH), lambda b, i: (b, i, 0)),
            pl.BlockSpec((1, CONV_HALO, CONV_WIDTH),
                         lambda b, i: (b, jnp.minimum((i + 1) * hpb, nhalo - 1), 0)),
            _resident((CONV_KERNEL, SUBLANES, CONV_WIDTH)), _resident((1, CONV_WIDTH)),
            _resident((1, CONV_WIDTH)), _resident((1, CONV_WIDTH))],
        out_specs=pl.BlockSpec((1, tc, CONV_WIDTH), lambda b, i: (b, i, 0)),
        out_shape=jax.ShapeDtypeStruct((batch, seq, CONV_WIDTH), BF16),
        scratch_shapes=[pltpu.VMEM((tc + 2 * CONV_HALO, CONV_WIDTH), F32)],
        compiler_params=_params("parallel", "parallel"),
        name="conv_branch",
    )(hg3, hg3, hg3, w_dw, b_dw, ln_g, ln_b)
    return out.reshape(batch * seq, CONV_WIDTH)


def _t5_bucket(rel):
    nb = NUM_BUCKETS // 2
    ret = jnp.where(rel > 0, nb, 0)
    n = jnp.abs(rel)
    max_exact = nb // 2
    nf = jnp.maximum(n, 1).astype(F32)
    large = max_exact + (jnp.log(nf / max_exact) / math.log(MAX_DISTANCE / max_exact)
                         * (nb - max_exact)).astype(jnp.int32)
    large = jnp.minimum(large, nb - 1)
    return ret + jnp.where(n < max_exact, n, large)


def _bias_kernel(tab_ref, o_ref, *, tile):
    h = pl.program_id(0)
    d = pl.program_id(1)
    nb = tile // BIAS_BLOCK
    key = lax.broadcasted_iota(jnp.int32, (BIAS_BLOCK, BIAS_BLOCK), 0)
    query = lax.broadcasted_iota(jnp.int32, (BIAS_BLOCK, BIAS_BLOCK), 1)

    def banded_block(o):
        bucket = _t5_bucket(key - query + o * BIAS_BLOCK)
        acc = jnp.zeros((BIAS_BLOCK, BIAS_BLOCK), F32)
        for b in range(NUM_BUCKETS):
            acc = jnp.where(bucket == b, tab_ref[b, h], acc)
        return acc * LOG2E

    below, diag, above = banded_block(-1), banded_block(0), banded_block(1)
    far_below = tab_ref[NUM_BUCKETS // 2 - 1, h] * LOG2E
    far_above = tab_ref[NUM_BUCKETS - 1, h] * LOG2E
    for kb in range(nb):
        for qb in range(nb):
            o = kb - qb + nb * (d - 1)
            block = jnp.where(o == -1, below, jnp.where(o == 0, diag, above))
            block = jnp.where(o <= -2, far_below, jnp.where(o >= 2, far_above, block))
            o_ref[0, 0, kb * BIAS_BLOCK:(kb + 1) * BIAS_BLOCK, qb * BIAS_BLOCK:(qb + 1) * BIAS_BLOCK] = block


def _bias_tiles(table, tile):
    return pl.pallas_call(
        functools.partial(_bias_kernel, tile=tile),
        grid=(N_HEADS, 3),
        in_specs=[pl.BlockSpec(memory_space=pltpu.SMEM)],
        out_specs=pl.BlockSpec((1, 1, tile, tile), lambda h, d: (h, d, 0, 0)),
        out_shape=jax.ShapeDtypeStruct((N_HEADS, 3, tile, tile), F32),
        compiler_params=_params("parallel", "parallel"),
        name="bias_tiles",
    )(table)


def _attn_kernel(tab_ref, qt_ref, k_ref, vt_ref, bias_ref, lq1_ref, lk1_ref, lq2_ref, lk2_ref,
                 o_ref, acc_ref, m_ref, s_ref, drift_ref, *, tile, nk, q_tiles):
    h = pl.program_id(1)

    def query_tile(sub, exact):
        q0 = pl.multiple_of(sub * tile, tile)
        begin, rest, finish = _attn_query_tile(
            h, pl.program_id(2) * q_tiles + sub, qt_ref[0, :, pl.ds(q0, tile)], tab_ref, k_ref, vt_ref,
            bias_ref, lq1_ref, lk1_ref, lq2_ref, lk2_ref, acc_ref, m_ref, s_ref, drift_ref,
            tile=tile, nk=nk, exact=exact)

        def store():
            o_ref[0, :, pl.ds(q0, tile)] = finish()

        return begin, rest, store

    drift_ref[...] = jnp.zeros(drift_ref.shape, F32)
    query_tile(0, False)[0]()

    def body(sub, carry):
        _, rest, store = query_tile(sub, False)
        rest()
        store()
        query_tile(sub + 1, False)[0]()
        return carry

    lax.fori_loop(0, q_tiles - 1, body, 0)
    _, rest, store = query_tile(q_tiles - 1, False)
    rest()
    store()

    @pl.when(jnp.max(drift_ref[...]) > MAX_EXPONENT_DRIFT)
    def _():
        def redo(sub, carry):
            begin, rest, store = query_tile(sub, True)
            begin()
            rest()
            store()
            return carry

        lax.fori_loop(0, q_tiles, redo, 0)


def _attn_query_tile(h, qi, qt, tab_ref, k_ref, vt_ref, bias_ref, lq1_ref, lk1_ref, lq2_ref, lk2_ref,
                     acc_ref, m_ref, s_ref, drift_ref, *, tile, nk, exact):
    row = lax.broadcasted_iota(jnp.int32, qt.shape, 0)
    zero = jnp.zeros_like(qt)
    qt_maps = (jnp.where(row < HEAD_DIM, qt, zero), jnp.where(row >= HEAD_DIM, qt, zero))
    ones = jnp.ones((ONES_ROWS, tile), BF16)

    def keys(ki):
        k0 = pl.multiple_of(ki * tile, tile)
        vt_aug = jnp.concatenate([vt_ref[0, :, pl.ds(k0, tile)], ones], axis=0)
        return k_ref[pl.ds(k0, tile), :], vt_aug

    def logits(k, mi, bias):
        s = _dot(k, qt_maps[mi])
        return s if bias is None else s + bias

    def exact_update(stage, mi):
        ki, bias, shift = stage
        k, vt_aug = keys(ki)
        s_ref[...] = logits(k, mi, bias)
        m_old = m_ref[mi]
        m_new = jnp.maximum(m_old, jnp.max(s_ref[...], axis=0, keepdims=True) + shift)
        p = jnp.exp2(s_ref[...] - (m_new - shift)).astype(BF16)
        acc_ref[mi] = jnp.exp2(m_old - m_new) * acc_ref[mi] + _dot(vt_aug, p)
        m_ref[mi] = m_new

    def streamed_update(stage, mi, first=False):
        ki, bias, shift = stage
        k, vt_aug = keys(ki)
        s = logits(k, mi, bias)
        m_old = jnp.max(s[:PROBE_KEYS], axis=0, keepdims=True) + shift if first else m_ref[mi]
        p = jnp.exp2(s - (m_old - shift)).astype(BF16)
        t_max = jnp.max(s, axis=0, keepdims=True) + shift
        m_new = jnp.maximum(m_old, t_max)
        pv = _dot(vt_aug, p)
        acc_ref[mi] = jnp.exp2(m_old - m_new) * (pv if first else acc_ref[mi] + pv)
        m_ref[mi] = m_new
        drift_ref[mi] = jnp.maximum(drift_ref[mi], t_max - m_old)

    far_shift = lambda ki: LOG2E * jnp.where(ki < qi, tab_ref[NUM_BUCKETS // 2 - 1, h],
                                             tab_ref[NUM_BUCKETS - 1, h])

    lo = jnp.maximum(qi - 1, 0)
    hi = jnp.minimum(qi + 2, nk)

    def near_stage(ki):
        return ki, bias_ref[0, ki - qi + 1], 0.0

    def far_stage(j):
        ki = jnp.where(j < lo, j, j + (hi - lo))
        return ki, None, far_shift(ki)

    def sweep(update, stage_of, start, stop, tiles_per_trip=1):
        def trip(width):
            def body(i, carry):
                for u in range(width):
                    stage = stage_of(start + i * width + u)
                    for mi in range(2):
                        update(stage, mi)
                return carry
            return body

        full = (stop - start) // tiles_per_trip
        lax.fori_loop(0, full, trip(tiles_per_trip), 0)
        if tiles_per_trip > 1:
            done = start + full * tiles_per_trip
            lax.fori_loop(0, stop - done, lambda i, c: trip(1)(done - start + i, c), 0)

    n_far = nk - (hi - lo)

    def begin():
        if exact:
            m_ref[...] = jnp.full(m_ref.shape, -jnp.inf, F32)
            acc_ref[...] = jnp.zeros(acc_ref.shape, F32)
            sweep(exact_update, near_stage, lo, lo + 1)
        else:
            for mi in range(2):
                streamed_update(near_stage(lo), mi, first=True)

    def rest():
        if exact:
            sweep(exact_update, near_stage, lo + 1, hi)
            sweep(exact_update, far_stage, 0, n_far)
        else:
            sweep(streamed_update, near_stage, lo + 1, hi)
            sweep(streamed_update, far_stage, 0, n_far, FAR_TILES_PER_TRIP)

    def finish():
        lam = (jnp.exp(jnp.sum(lq1_ref[...] * lk1_ref[...], keepdims=True))
               - jnp.exp(jnp.sum(lq2_ref[...] * lk2_ref[...], keepdims=True)) + LAMBDA_INIT)
        o1 = acc_ref[0]
        o2 = acc_ref[1]
        return o1[:V_DIM] / o1[V_DIM:V_DIM + 1] - lam * (o2[:V_DIM] / o2[V_DIM:V_DIM + 1])

    return begin, rest, finish


def _attention(qt, k, vt, table, bias, lq1, lk1, lq2, lk2, batch, seq, tile):
    nk = seq // tile
    q_tiles = math.gcd(nk, QUERY_TILES_PER_STEP)
    assert seq % tile == 0 and tile % BIAS_BLOCK == 0 and BIAS_BLOCK >= MAX_DISTANCE
    small = _resident((1, HEAD_DIM))
    slab = pl.BlockSpec((1, V_DIM, q_tiles * tile), lambda b, h, i: (b, h, i))
    return pl.pallas_call(
        functools.partial(_attn_kernel, tile=tile, nk=nk, q_tiles=q_tiles),
        grid=(batch, N_HEADS, nk // q_tiles),
        in_specs=[
            pl.BlockSpec(memory_space=pltpu.SMEM),
            slab,
            pl.BlockSpec((seq, V_DIM), lambda b, h, i: (b, h), pipeline_mode=pl.Buffered(1)),
            pl.BlockSpec((1, V_DIM, seq), lambda b, h, i: (b, h, 0), pipeline_mode=pl.Buffered(1)),
            pl.BlockSpec((1, 3, tile, tile), lambda b, h, i: (h, 0, 0, 0), pipeline_mode=pl.Buffered(1)),
            small, small, small, small],
        out_specs=slab,
        out_shape=jax.ShapeDtypeStruct((batch, COL_V, seq), F32),
        scratch_shapes=[pltpu.VMEM((2, V_DIM + ONES_ROWS, tile), F32), pltpu.VMEM((2, 1, tile), F32),
                        pltpu.VMEM((tile, tile), F32), pltpu.VMEM((2, 1, tile), F32)],
        compiler_params=_params("parallel", "parallel", "parallel"),
        name="diff_attention",
    )(table, qt, k, vt, bias, lq1, lk1, lq2, lk2)


def _merge_ln_kernel(x_ref, hc_ref, at_ref, sg_ref, wco_ref, wao_ref, wg_ref, bg_ref, wo_ref, g_ref, b_ref,
                     o_ref):
    x = x_ref[...]
    xb = x.astype(BF16)
    y_conv = _dot(hc_ref[...], wco_ref[...])
    heads = []
    for hd in range(N_HEADS):
        a = at_ref[0, hd * V_DIM:(hd + 1) * V_DIM, :]
        r = a * lax.rsqrt(jnp.mean(a * a, axis=0, keepdims=True) + LN_EPS) * sg_ref[...]
        heads.append(r * (1.0 - LAMBDA_INIT))
    att = jnp.concatenate(heads, axis=0).T.astype(BF16)
    y_attn = _dot(att, wao_ref[...])
    g_conv = jax.nn.sigmoid(_dot(xb, wg_ref[:, :D_MODEL]) + bg_ref[:, :D_MODEL])
    g_attn = jax.nn.sigmoid(_dot(xb, wg_ref[:, D_MODEL:]) + bg_ref[:, D_MODEL:])
    merged = g_conv * y_conv + g_attn * y_attn
    m = _dot(merged.astype(BF16), wo_ref[...])
    o_ref[...] = _layer_norm(ALPHA * x + m, g_ref[...], b_ref[...])


def _merge_ln(x1, hc, att_t, subln_g, wco, wao, wg, bg, wo, g, b, seq):
    t = x1.shape[0]
    tm = min(TOKEN_TILE, seq)
    per_seq = seq // tm
    row = pl.BlockSpec((tm, D_MODEL), lambda i: (i, 0))
    half = pl.BlockSpec((tm, CONV_WIDTH), lambda i: (i, 0))
    return pl.pallas_call(
        _merge_ln_kernel,
        grid=(t // tm,),
        in_specs=[row, half, pl.BlockSpec((1, COL_V, tm), lambda i: (i // per_seq, 0, i % per_seq)),
                  _resident((V_DIM, 1)), _resident((CONV_WIDTH, D_MODEL)), _resident((ATTN_WIDTH, D_MODEL)),
                  _resident((D_MODEL, 2 * D_MODEL)), _resident((1, 2 * D_MODEL)),
                  _resident((D_MODEL, D_MODEL)), _resident((1, D_MODEL)), _resident((1, D_MODEL))],
        out_specs=row,
        out_shape=jax.ShapeDtypeStruct((t, D_MODEL), F32),
        compiler_params=_params("parallel"),
        name="merge_ln",
    )(x1, hc, att_t, subln_g, wco, wao, wg, bg, wo, g, b)


def _layer(x, p, bias, tile):
    batch, seq, _ = x.shape
    x0 = x.reshape(batch * seq, D_MODEL)
    x1 = _ffn_ln(x0, p["ffn1_wg"], p["ffn1_wu"], p["ffn1_wd"], p["ln1_g"], p["ln1_b"])
    hg, qt, k, vt = _in_proj(x1, p["w_uqkv"], batch, seq)
    hc = _conv_branch(hg, p["conv_w"], p["conv_b"], p["conv_ln_g"], p["conv_ln_b"], batch, seq)
    att_t = _attention(qt, k, vt, p["table"], bias, p["lq1"], p["lk1"], p["lq2"], p["lk2"], batch, seq, tile)
    x2 = _merge_ln(x1, hc, att_t, p["subln_g"], p["w_conv_out"], p["w_attn_out"], p["w_gate"], p["b_gate"],
                   p["w_o"], p["ln2_g"], p["ln2_b"], seq)
    x3 = _ffn_ln(x2, p["ffn2_wg"], p["ffn2_wu"], p["ffn2_wd"], p["ln3_g"], p["ln3_b"])
    return x3.reshape(batch, seq, D_MODEL)


def _prepare(rel_bias_table, ffn1_w_gu, ffn1_w_down, ln1_g, ln1_b, w_in, b_gate, conv_w_dw, conv_b_dw,
             conv_ln_g, conv_ln_b, w_conv_out, lambda_q1, lambda_k1, lambda_q2, lambda_k2, subln_g,
             w_attn_out, w_o, ln2_g, ln2_b, ffn2_w_gu, ffn2_w_down, ln3_g, ln3_b):
    l = 0
    n_uqkv = COL_U + COL_Q + COL_K + COL_V
    row = lambda a: a[l].reshape(1, -1).astype(F32)
    return {
        "table": rel_bias_table.astype(F32),
        "ffn1_wg": ffn1_w_gu[l, :, :D_FF].astype(BF16), "ffn1_wu": ffn1_w_gu[l, :, D_FF:].astype(BF16),
        "ffn1_wd": ffn1_w_down[l].astype(BF16), "ln1_g": row(ln1_g), "ln1_b": row(ln1_b),
        "w_uqkv": w_in[l, :, :n_uqkv].astype(BF16), "w_gate": w_in[l, :, n_uqkv:].astype(BF16),
        "b_gate": row(b_gate),
        "conv_w": jnp.broadcast_to(conv_w_dw[l].reshape(CONV_KERNEL, 1, CONV_WIDTH).astype(F32),
                                   (CONV_KERNEL, SUBLANES, CONV_WIDTH)),
        "conv_b": row(conv_b_dw),
        "conv_ln_g": row(conv_ln_g), "conv_ln_b": row(conv_ln_b),
        "w_conv_out": w_conv_out[l].astype(BF16),
        "lq1": row(lambda_q1), "lk1": row(lambda_k1), "lq2": row(lambda_q2), "lk2": row(lambda_k2),
        "subln_g": subln_g[l].reshape(-1, 1).astype(F32), "w_attn_out": w_attn_out[l].astype(BF16), "w_o": w_o[l].astype(BF16),
        "ln2_g": row(ln2_g), "ln2_b": row(ln2_b),
        "ffn2_wg": ffn2_w_gu[l, :, :D_FF].astype(BF16), "ffn2_wu": ffn2_w_gu[l, :, D_FF:].astype(BF16),
        "ffn2_wd": ffn2_w_down[l].astype(BF16), "ln3_g": row(ln3_g), "ln3_b": row(ln3_b),
    }


def kernel(x_prompt, x_sample, rel_bias_table, ffn1_w_gu, ffn1_w_down, ln1_g, ln1_b, w_in, b_gate, conv_w_dw, conv_b_dw, conv_ln_g, conv_ln_b, w_conv_out, lambda_q1, lambda_k1, lambda_q2, lambda_k2, subln_g, w_attn_out, w_o, ln2_g, ln2_b, ffn2_w_gu, ffn2_w_down, ln3_g, ln3_b):
    p = _prepare(rel_bias_table, ffn1_w_gu, ffn1_w_down, ln1_g, ln1_b, w_in, b_gate, conv_w_dw, conv_b_dw,
                 conv_ln_g, conv_ln_b, w_conv_out, lambda_q1, lambda_k1, lambda_q2, lambda_k2, subln_g,
                 w_attn_out, w_o, ln2_g, ln2_b, ffn2_w_gu, ffn2_w_down, ln3_g, ln3_b)
    tile = min(ATTN_TILE, x_prompt.shape[1], x_sample.shape[1])
    bias = _bias_tiles(p["table"], tile)
    return (_layer(x_prompt, p, bias, tile), _layer(x_sample, p, bias, tile))
```

```python
import functools
import math

import jax
import jax.numpy as jnp
from jax import lax
from jax.experimental import pallas as pl
from jax.experimental.pallas import tpu as pltpu

F32 = jnp.float32
BF16 = jnp.bfloat16

D_MODEL = 1024
N_HEADS = 4
HEAD_DIM = 64
V_DIM = 2 * HEAD_DIM
ATTN_WIDTH = N_HEADS * 2 * HEAD_DIM
CONV_WIDTH = 512
CONV_KERNEL = 31
CONV_HALO = 16
D_FF = 2816
NUM_BUCKETS = 32
MAX_DISTANCE = 128
DEPTH = 1
ALPHA = (2.0 * DEPTH) ** 0.25
LN_EPS = 1e-5
ATTN_SCALE = HEAD_DIM ** -0.5
LAMBDA_INIT = 0.8 - 0.6 * math.exp(-0.3 * 0)
LOG2E = math.log2(math.e)
ONES_ROWS = 16
COL_U = 2 * CONV_WIDTH
COL_Q = ATTN_WIDTH
COL_K = ATTN_WIDTH
COL_V = N_HEADS * V_DIM

V7X_VMEM_LIMIT_BYTES = 56 * 1024 * 1024
FF_CHUNK = 256
TOKEN_TILE = 1024
EPILOGUE_PIECES = 4
CONV_TILE = 256
CONV_ROWS = 64
CONV_LANES = 256
SUBLANES = 8
ATTN_TILE = 1024
FAR_TILES_PER_TRIP = 2
QUERY_TILES_PER_STEP = 8
MAX_EXPONENT_DRIFT = 64.0
PROBE_KEYS = 16
BIAS_BLOCK = 128


def _dot(a, b):
    return jnp.dot(a, b, preferred_element_type=F32)


def _layer_norm(z, g, b):
    mu = jnp.mean(z, axis=-1, keepdims=True)
    zc = z - mu
    var = jnp.mean(zc * zc, axis=-1, keepdims=True)
    return zc * lax.rsqrt(var + LN_EPS) * g + b


def _resident(shape):
    return pl.BlockSpec(shape, lambda *_: (0,) * len(shape), pipeline_mode=pl.Buffered(1))


def _params(*semantics):
    return pltpu.CompilerParams(dimension_semantics=semantics,
                                vmem_limit_bytes=V7X_VMEM_LIMIT_BYTES)


def _ffn_ln_kernel(x_ref, wg_ref, wu_ref, wd_ref, g_ref, b_ref, o_ref, h_ref):
    x = x_ref[...]
    xb = x.astype(BF16)
    for c in range(D_FF // FF_CHUNK):
        sl = slice(c * FF_CHUNK, (c + 1) * FF_CHUNK)
        a = _dot(xb, wg_ref[:, sl])
        u = _dot(xb, wu_ref[:, sl])
        h_ref[:, sl] = (a * jax.nn.sigmoid(a) * u).astype(BF16)
    rows = x.shape[0] // EPILOGUE_PIECES
    for r in range(EPILOGUE_PIECES):
        rs = slice(r * rows, (r + 1) * rows)
        y = _dot(h_ref[rs, :], wd_ref[...])
        o_ref[rs, :] = _layer_norm(ALPHA * x[rs] + 0.5 * y, g_ref[...], b_ref[...])


def _ffn_ln(x, wg, wu, wd, g, b):
    t = x.shape[0]
    tm = min(TOKEN_TILE, t)
    row = pl.BlockSpec((tm, D_MODEL), lambda i: (i, 0))
    return pl.pallas_call(
        _ffn_ln_kernel,
        grid=(t // tm,),
        in_specs=[row, _resident((D_MODEL, D_FF)), _resident((D_MODEL, D_FF)),
                  _resident((D_FF, D_MODEL)), _resident((1, D_MODEL)), _resident((1, D_MODEL))],
        out_specs=row,
        out_shape=jax.ShapeDtypeStruct((t, D_MODEL), F32),
        scratch_shapes=[pltpu.VMEM((tm, D_FF), BF16)],
        compiler_params=_params("parallel"),
        name="ffn_ln",
    )(x, wg, wu, wd, g, b)


def _in_proj_kernel(x_ref, w_ref, hg_ref, qt_ref, k_ref, vt_ref):
    xb = x_ref[...].astype(BF16)
    u = _dot(xb, w_ref[:, 0:COL_U])
    hg_ref[...] = u[:, :CONV_WIDTH] * jax.nn.sigmoid(u[:, CONV_WIDTH:])
    o = COL_U
    qt_ref[0] = (_dot(xb, w_ref[:, o:o + COL_Q]) * (ATTN_SCALE * LOG2E)).T.astype(BF16)
    o += COL_Q
    k_ref[...] = _dot(xb, w_ref[:, o:o + COL_K]).astype(BF16)
    o += COL_K
    vt_ref[0] = _dot(xb, w_ref[:, o:o + COL_V]).T.astype(BF16)


def _in_proj(x1, w_uqkv, batch, seq):
    t = x1.shape[0]
    tm = min(TOKEN_TILE, seq)
    per_seq = seq // tm
    ncol = COL_U + COL_Q + COL_K + COL_V
    return pl.pallas_call(
        _in_proj_kernel,
        grid=(t // tm,),
        in_specs=[pl.BlockSpec((tm, D_MODEL), lambda i: (i, 0)), _resident((D_MODEL, ncol))],
        out_specs=[pl.BlockSpec((tm, CONV_WIDTH), lambda i: (i, 0)),
                   pl.BlockSpec((1, ATTN_WIDTH, tm), lambda i: (i // per_seq, 0, i % per_seq)),
                   pl.BlockSpec((tm, ATTN_WIDTH), lambda i: (i, 0)),
                   pl.BlockSpec((1, COL_V, tm), lambda i: (i // per_seq, 0, i % per_seq))],
        out_shape=[jax.ShapeDtypeStruct((t, CONV_WIDTH), F32),
                   jax.ShapeDtypeStruct((batch, ATTN_WIDTH, seq), BF16),
                   jax.ShapeDtypeStruct((t, ATTN_WIDTH), BF16),
                   jax.ShapeDtypeStruct((batch, COL_V, seq), BF16)],
        compiler_params=_params("parallel"),
        name="in_proj",
    )(x1, w_uqkv)


def _conv_kernel(prev_ref, cur_ref, next_ref, w_ref, b_ref, g_ref, beta_ref, o_ref, ext_ref, *, tc):
    i = pl.program_id(1)
    last = pl.num_programs(1) - 1
    ext_ref[0:CONV_HALO, :] = jnp.where(i > 0, prev_ref[0], 0.0)
    ext_ref[CONV_HALO:CONV_HALO + tc, :] = cur_ref[0]
    ext_ref[CONV_HALO + tc:2 * CONV_HALO + tc, :] = jnp.where(i < last, next_ref[0], 0.0)
    first_tap = CONV_HALO - CONV_KERNEL // 2
    for r in range(tc // CONV_ROWS):
        r0 = r * CONV_ROWS
        halves = []
        for c in range(CONV_WIDTH // CONV_LANES):
            cs = slice(c * CONV_LANES, (c + 1) * CONV_LANES)
            acc = None
            for b in range(SUBLANES):
                u = None
                for j in range(CONV_KERNEL):
                    if (first_tap + j) % SUBLANES != b:
                        continue
                    e0 = r0 + first_tap + j - b
                    rows = ext_ref[e0:e0 + CONV_ROWS + SUBLANES, cs].reshape(-1, SUBLANES, CONV_LANES)
                    term = rows * w_ref[j, :, cs]
                    u = term if u is None else u + term
                u = u.reshape(CONV_ROWS + SUBLANES, CONV_LANES)[b:b + CONV_ROWS]
                acc = u if acc is None else acc + u
            halves.append(acc)
        y = _layer_norm(jnp.concatenate(halves, axis=1) + b_ref[...], g_ref[...], beta_ref[...])
        o_ref[0, r0:r0 + CONV_ROWS, :] = (y * jax.nn.sigmoid(y)).astype(BF16)


def _conv_branch(hg, w_dw, b_dw, ln_g, ln_b, batch, seq):
    tc = min(CONV_TILE, seq)
    hpb = tc // CONV_HALO
    nhalo = seq // CONV_HALO
    hg3 = hg.reshape(batch, seq, CONV_WIDTH)
    out = pl.pallas_call(
        functools.partial(_conv_kernel, tc=tc),
        grid=(batch, seq // tc),
        in_specs=[
            pl.BlockSpec((1, CONV_HALO, CONV_WIDTH), lambda b, i: (b, jnp.maximum(i * hpb - 1, 0), 0)),
            pl.BlockSpec((1, tc, CONV_WIDTH), lambda b, i: (b, i, 0)),
            pl.BlockSpec((1, CONV_HALO, CONV_WIDTH),
                         lambda b, i: (b, jnp.minimum((i + 1) * hpb, nhalo - 1), 0)),
            _resident((CONV_KERNEL, SUBLANES, CONV_WIDTH)), _resident((1, CONV_WIDTH)),
            _resident((1, CONV_WIDTH)), _resident((1, CONV_WIDTH))],
        out_specs=pl.BlockSpec((1, tc, CONV_WIDTH), lambda b, i: (b, i, 0)),
        out_shape=jax.ShapeDtypeStruct((batch, seq, CONV_WIDTH), BF16),
        scratch_shapes=[pltpu.VMEM((tc + 2 * CONV_HALO, CONV_WIDTH), F32)],
        compiler_params=_params("parallel", "parallel"),
        name="conv_branch",
    )(hg3, hg3, hg3, w_dw, b_dw, ln_g, ln_b)
    return out.reshape(batch * seq, CONV_WIDTH)


def _t5_bucket(rel):
    nb = NUM_BUCKETS // 2
    ret = jnp.where(rel > 0, nb, 0)
    n = jnp.abs(rel)
    max_exact = nb // 2
    nf = jnp.maximum(n, 1).astype(F32)
    large = max_exact + (jnp.log(nf / max_exact) / math.log(MAX_DISTANCE / max_exact)
                         * (nb - max_exact)).astype(jnp.int32)
    large = jnp.minimum(large, nb - 1)
    return ret + jnp.where(n < max_exact, n, large)


def _bias_kernel(tab_ref, o_ref, *, tile):
    h = pl.program_id(0)
    d = pl.program_id(1)
    nb = tile // BIAS_BLOCK
    key = lax.broadcasted_iota(jnp.int32, (BIAS_BLOCK, BIAS_BLOCK), 0)
    query = lax.broadcasted_iota(jnp.int32, (BIAS_BLOCK, BIAS_BLOCK), 1)

    def banded_block(o):
        bucket = _t5_bucket(key - query + o * BIAS_BLOCK)
        acc = jnp.zeros((BIAS_BLOCK, BIAS_BLOCK), F32)
        for b in range(NUM_BUCKETS):
            acc = jnp.where(bucket == b, tab_ref[b, h], acc)
        return acc * LOG2E

    below, diag, above = banded_block(-1), banded_block(0), banded_block(1)
    far_below = tab_ref[NUM_BUCKETS // 2 - 1, h] * LOG2E
    far_above = tab_ref[NUM_BUCKETS - 1, h] * LOG2E
    for kb in range(nb):
        for qb in range(nb):
            o = kb - qb + nb * (d - 1)
            block = jnp.where(o == -1, below, jnp.where(o == 0, diag, above))
            block = jnp.where(o <= -2, far_below, jnp.where(o >= 2, far_above, block))
            o_ref[0, 0, kb * BIAS_BLOCK:(kb + 1) * BIAS_BLOCK, qb * BIAS_BLOCK:(qb + 1) * BIAS_BLOCK] = block


def _bias_tiles(table, tile):
    return pl.pallas_call(
        functools.partial(_bias_kernel, tile=tile),
        grid=(N_HEADS, 3),
        in_specs=[pl.BlockSpec(memory_space=pltpu.SMEM)],
        out_specs=pl.BlockSpec((1, 1, tile, tile), lambda h, d: (h, d, 0, 0)),
        out_shape=jax.ShapeDtypeStruct((N_HEADS, 3, tile, tile), F32),
        compiler_params=_params("parallel", "parallel"),
        name="bias_tiles",
    )(table)


def _attn_kernel(tab_ref, qt_ref, k_ref, vt_ref, bias_ref, lq1_ref, lk1_ref, lq2_ref, lk2_ref,
                 o_ref, acc_ref, m_ref, s_ref, drift_ref, *, tile, nk, q_tiles):
    h = pl.program_id(1)

    def query_tile(sub, exact):
        q0 = pl.multiple_of(sub * tile, tile)
        begin, rest, finish = _attn_query_tile(
            h, pl.program_id(2) * q_tiles + sub, qt_ref[0, :, pl.ds(q0, tile)], tab_ref, k_ref, vt_ref,
            bias_ref, lq1_ref, lk1_ref, lq2_ref, lk2_ref, acc_ref, m_ref, s_ref, drift_ref,
            tile=tile, nk=nk, exact=exact)

        def store():
            o_ref[0, :, pl.ds(q0, tile)] = finish()

        return begin, rest, store

    drift_ref[...] = jnp.zeros(drift_ref.shape, F32)
    query_tile(0, False)[0]()

    def body(sub, carry):
        _, rest, store = query_tile(sub, False)
        rest()
        store()
        query_tile(sub + 1, False)[0]()
        return carry

    lax.fori_loop(0, q_tiles - 1, body, 0)
    _, rest, store = query_tile(q_tiles - 1, False)
    rest()
    store()

    @pl.when(jnp.max(drift_ref[...]) > MAX_EXPONENT_DRIFT)
    def _():
        def redo(sub, carry):
            begin, rest, store = query_tile(sub, True)
            begin()
            rest()
            store()
            return carry

        lax.fori_loop(0, q_tiles, redo, 0)


def _attn_query_tile(h, qi, qt, tab_ref, k_ref, vt_ref, bias_ref, lq1_ref, lk1_ref, lq2_ref, lk2_ref,
                     acc_ref, m_ref, s_ref, drift_ref, *, tile, nk, exact):
    row = lax.broadcasted_iota(jnp.int32, qt.shape, 0)
    zero = jnp.zeros_like(qt)
    qt_maps = (jnp.where(row < HEAD_DIM, qt, zero), jnp.where(row >= HEAD_DIM, qt, zero))
    ones = jnp.ones((ONES_ROWS, tile), BF16)

    def keys(ki):
        k0 = pl.multiple_of(ki * tile, tile)
        vt_aug = jnp.concatenate([vt_ref[0, :, pl.ds(k0, tile)], ones], axis=0)
        return k_ref[pl.ds(k0, tile), :], vt_aug

    def logits(k, mi, bias):
        s = _dot(k, qt_maps[mi])
        return s if bias is None else s + bias

    def exact_update(stage, mi):
        ki, bias, shift = stage
        k, vt_aug = keys(ki)
        s_ref[...] = logits(k, mi, bias)
        m_old = m_ref[mi]
        m_new = jnp.maximum(m_old, jnp.max(s_ref[...], axis=0, keepdims=True) + shift)
        p = jnp.exp2(s_ref[...] - (m_new - shift)).astype(BF16)
        acc_ref[mi] = jnp.exp2(m_old - m_new) * acc_ref[mi] + _dot(vt_aug, p)
        m_ref[mi] = m_new

    def streamed_update(stage, mi, first=False):
        ki, bias, shift = stage
        k, vt_aug = keys(ki)
        s = logits(k, mi, bias)
        m_old = jnp.max(s[:PROBE_KEYS], axis=0, keepdims=True) + shift if first else m_ref[mi]
        p = jnp.exp2(s - (m_old - shift)).astype(BF16)
        t_max = jnp.max(s, axis=0, keepdims=True) + shift
        m_new = jnp.maximum(m_old, t_max)
        pv = _dot(vt_aug, p)
        acc_ref[mi] = jnp.exp2(m_old - m_new) * (pv if first else acc_ref[mi] + pv)
        m_ref[mi] = m_new
        drift_ref[mi] = jnp.maximum(drift_ref[mi], t_max - m_old)

    far_shift = lambda ki: LOG2E * jnp.where(ki < qi, tab_ref[NUM_BUCKETS // 2 - 1, h],
                                             tab_ref[NUM_BUCKETS - 1, h])

    lo = jnp.maximum(qi - 1, 0)
    hi = jnp.minimum(qi + 2, nk)

    def near_stage(ki):
        return ki, bias_ref[0, ki - qi + 1], 0.0

    def far_stage(j):
        ki = jnp.where(j < lo, j, j + (hi - lo))
        return ki, None, far_shift(ki)

    def sweep(update, stage_of, start, stop, tiles_per_trip=1):
        def trip(width):
            def body(i, carry):
                for u in range(width):
                    stage = stage_of(start + i * width + u)
                    for mi in range(2):
                        update(stage, mi)
                return carry
            return body

        full = (stop - start) // tiles_per_trip
        lax.fori_loop(0, full, trip(tiles_per_trip), 0)
        if tiles_per_trip > 1:
            done = start + full * tiles_per_trip
            lax.fori_loop(0, stop - done, lambda i, c: trip(1)(done - start + i, c), 0)

    n_far = nk - (hi - lo)

    def begin():
        if exact:
            m_ref[...] = jnp.full(m_ref.shape, -jnp.inf, F32)
            acc_ref[...] = jnp.zeros(acc_ref.shape, F32)
            sweep(exact_update, near_stage, lo, lo + 1)
        else:
            for mi in range(2):
                streamed_update(near_stage(lo), mi, first=True)

    def rest():
        if exact:
            sweep(exact_update, near_stage, lo + 1, hi)
            sweep(exact_update, far_stage, 0, n_far)
        else:
            sweep(streamed_update, near_stage, lo + 1, hi)
            sweep(streamed_update, far_stage, 0, n_far, FAR_TILES_PER_TRIP)

    def finish():
        lam = (jnp.exp(jnp.sum(lq1_ref[...] * lk1_ref[...], keepdims=True))
               - jnp.exp(jnp.sum(lq2_ref[...] * lk2_ref[...], keepdims=True)) + LAMBDA_INIT)
        o1 = acc_ref[0]
        o2 = acc_ref[1]
        return o1[:V_DIM] / o1[V_DIM:V_DIM + 1] - lam * (o2[:V_DIM] / o2[V_DIM:V_DIM + 1])

    return begin, rest, finish


def _attention(qt, k, vt, table, bias, lq1, lk1, lq2, lk2, batch, seq, tile):
    nk = seq // tile
    q_tiles = math.gcd(nk, QUERY_TILES_PER_STEP)
    assert seq % tile == 0 and tile % BIAS_BLOCK == 0 and BIAS_BLOCK >= MAX_DISTANCE
    small = _resident((1, HEAD_DIM))
    slab = pl.BlockSpec((1, V_DIM, q_tiles * tile), lambda b, h, i: (b, h, i))
    return pl.pallas_call(
        functools.partial(_attn_kernel, tile=tile, nk=nk, q_tiles=q_tiles),
        grid=(batch, N_HEADS, nk // q_tiles),
        in_specs=[
            pl.BlockSpec(memory_space=pltpu.SMEM),
            slab,
            pl.BlockSpec((seq, V_DIM), lambda b, h, i: (b, h), pipeline_mode=pl.Buffered(1)),
            pl.BlockSpec((1, V_DIM, seq), lambda b, h, i: (b, h, 0), pipeline_mode=pl.Buffered(1)),
            pl.BlockSpec((1, 3, tile, tile), lambda b, h, i: (h, 0, 0, 0), pipeline_mode=pl.Buffered(1)),
            small, small, small, small],
        out_specs=slab,
        out_shape=jax.ShapeDtypeStruct((batch, COL_V, seq), F32),
        scratch_shapes=[pltpu.VMEM((2, V_DIM + ONES_ROWS, tile), F32), pltpu.VMEM((2, 1, tile), F32),
                        pltpu.VMEM((tile, tile), F32), pltpu.VMEM((2, 1, tile), F32)],
        compiler_params=_params("parallel", "parallel", "parallel"),
        name="diff_attention",
    )(table, qt, k, vt, bias, lq1, lk1, lq2, lk2)


def _merge_ln_kernel(x_ref, hc_ref, at_ref, sg_ref, wco_ref, wao_ref, wg_ref, bg_ref, wo_ref, g_ref, b_ref,
                     o_ref):
    x = x_ref[...]
    xb = x.astype(BF16)
    y_conv = _dot(hc_ref[...], wco_ref[...])
    heads = []
    for hd in range(N_HEADS):
        a = at_ref[0, hd * V_DIM:(hd + 1) * V_DIM, :]
        r = a * lax.rsqrt(jnp.mean(a * a, axis=0, keepdims=True) + LN_EPS) * sg_ref[...]
        heads.append(r * (1.0 - LAMBDA_INIT))
    att = jnp.concatenate(heads, axis=0).T.astype(BF16)
    y_attn = _dot(att, wao_ref[...])
    g_conv = jax.nn.sigmoid(_dot(xb, wg_ref[:, :D_MODEL]) + bg_ref[:, :D_MODEL])
    g_attn = jax.nn.sigmoid(_dot(xb, wg_ref[:, D_MODEL:]) + bg_ref[:, D_MODEL:])
    merged = (g_conv * y_conv + g_attn * y_attn).astype(BF16)
    rows = x.shape[0] // EPILOGUE_PIECES
    for r in range(EPILOGUE_PIECES):
        rs = slice(r * rows, (r + 1) * rows)
        m = _dot(merged[rs], wo_ref[...])
        o_ref[rs, :] = _layer_norm(ALPHA * x[rs] + m, g_ref[...], b_ref[...])


def _merge_ln(x1, hc, att_t, subln_g, wco, wao, wg, bg, wo, g, b, seq):
    t = x1.shape[0]
    tm = min(TOKEN_TILE, seq)
    per_seq = seq // tm
    row = pl.BlockSpec((tm, D_MODEL), lambda i: (i, 0))
    half = pl.BlockSpec((tm, CONV_WIDTH), lambda i: (i, 0))
    return pl.pallas_call(
        _merge_ln_kernel,
        grid=(t // tm,),
        in_specs=[row, half, pl.BlockSpec((1, COL_V, tm), lambda i: (i // per_seq, 0, i % per_seq)),
                  _resident((V_DIM, 1)), _resident((CONV_WIDTH, D_MODEL)), _resident((ATTN_WIDTH, D_MODEL)),
                  _resident((D_MODEL, 2 * D_MODEL)), _resident((1, 2 * D_MODEL)),
                  _resident((D_MODEL, D_MODEL)), _resident((1, D_MODEL)), _resident((1, D_MODEL))],
        out_specs=row,
        out_shape=jax.ShapeDtypeStruct((t, D_MODEL), F32),
        compiler_params=_params("parallel"),
        name="merge_ln",
    )(x1, hc, att_t, subln_g, wco, wao, wg, bg, wo, g, b)


def _layer(x, p, bias, tile):
    batch, seq, _ = x.shape
    x0 = x.reshape(batch * seq, D_MODEL)
    x1 = _ffn_ln(x0, p["ffn1_wg"], p["ffn1_wu"], p["ffn1_wd"], p["ln1_g"], p["ln1_b"])
    hg, qt, k, vt = _in_proj(x1, p["w_uqkv"], batch, seq)
    hc = _conv_branch(hg, p["conv_w"], p["conv_b"], p["conv_ln_g"], p["conv_ln_b"], batch, seq)
    att_t = _attention(qt, k, vt, p["table"], bias, p["lq1"], p["lk1"], p["lq2"], p["lk2"], batch, seq, tile)
    x2 = _merge_ln(x1, hc, att_t, p["subln_g"], p["w_conv_out"], p["w_attn_out"], p["w_gate"], p["b_gate"],
                   p["w_o"], p["ln2_g"], p["ln2_b"], seq)
    x3 = _ffn_ln(x2, p["ffn2_wg"], p["ffn2_wu"], p["ffn2_wd"], p["ln3_g"], p["ln3_b"])
    return x3.reshape(batch, seq, D_MODEL)


def _prepare(rel_bias_table, ffn1_w_gu, ffn1_w_down, ln1_g, ln1_b, w_in, b_gate, conv_w_dw, conv_b_dw,
             conv_ln_g, conv_ln_b, w_conv_out, lambda_q1, lambda_k1, lambda_q2, lambda_k2, subln_g,
             w_attn_out, w_o, ln2_g, ln2_b, ffn2_w_gu, ffn2_w_down, ln3_g, ln3_b):
    l = 0
    n_uqkv = COL_U + COL_Q + COL_K + COL_V
    row = lambda a: a[l].reshape(1, -1).astype(F32)
    return {
        "table": rel_bias_table.astype(F32),
        "ffn1_wg": ffn1_w_gu[l, :, :D_FF].astype(BF16), "ffn1_wu": ffn1_w_gu[l, :, D_FF:].astype(BF16),
        "ffn1_wd": ffn1_w_down[l].astype(BF16), "ln1_g": row(ln1_g), "ln1_b": row(ln1_b),
        "w_uqkv": w_in[l, :, :n_uqkv].astype(BF16), "w_gate": w_in[l, :, n_uqkv:].astype(BF16),
        "b_gate": row(b_gate),
        "conv_w": jnp.broadcast_to(conv_w_dw[l].reshape(CONV_KERNEL, 1, CONV_WIDTH).astype(F32),
                                   (CONV_KERNEL, SUBLANES, CONV_WIDTH)),
        "conv_b": row(conv_b_dw),
        "conv_ln_g": row(conv_ln_g), "conv_ln_b": row(conv_ln_b),
        "w_conv_out": w_conv_out[l].astype(BF16),
        "lq1": row(lambda_q1), "lk1": row(lambda_k1), "lq2": row(lambda_q2), "lk2": row(lambda_k2),
        "subln_g": subln_g[l].reshape(-1, 1).astype(F32), "w_attn_out": w_attn_out[l].astype(BF16), "w_o": w_o[l].astype(BF16),
        "ln2_g": row(ln2_g), "ln2_b": row(ln2_b),
        "ffn2_wg": ffn2_w_gu[l, :, :D_FF].astype(BF16), "ffn2_wu": ffn2_w_gu[l, :, D_FF:].astype(BF16),
        "ffn2_wd": ffn2_w_down[l].astype(BF16), "ln3_g": row(ln3_g), "ln3_b": row(ln3_b),
    }


def kernel(x_prompt, x_sample, rel_bias_table, ffn1_w_gu, ffn1_w_down, ln1_g, ln1_b, w_in, b_gate, conv_w_dw, conv_b_dw, conv_ln_g, conv_ln_b, w_conv_out, lambda_q1, lambda_k1, lambda_q2, lambda_k2, subln_g, w_attn_out, w_o, ln2_g, ln2_b, ffn2_w_gu, ffn2_w_down, ln3_g, ln3_b):
    p = _prepare(rel_bias_table, ffn1_w_gu, ffn1_w_down, ln1_g, ln1_b, w_in, b_gate, conv_w_dw, conv_b_dw,
                 conv_ln_g, conv_ln_b, w_conv_out, lambda_q1, lambda_k1, lambda_q2, lambda_k2, subln_g,
                 w_attn_out, w_o, ln2_g, ln2_b, ffn2_w_gu, ffn2_w_down, ln3_g, ln3_b)
    tile = min(ATTN_TILE, x_prompt.shape[1], x_sample.shape[1])
    bias = _bias_tiles(p["table"], tile)
    return (_layer(x_prompt, p, bias, tile), _layer(x_sample, p, bias, tile))
```

```python
import functools
import math

import jax
import jax.numpy as jnp
from jax import lax
from jax.experimental import pallas as pl
from jax.experimental.pallas import tpu as pltpu

F32 = jnp.float32
BF16 = jnp.bfloat16

D_MODEL = 1024
N_HEADS = 4
HEAD_DIM = 64
V_DIM = 2 * HEAD_DIM
ATTN_WIDTH = N_HEADS * 2 * HEAD_DIM
CONV_WIDTH = 512
CONV_KERNEL = 31
CONV_HALO = 16
D_FF = 2816
NUM_BUCKETS = 32
MAX_DISTANCE = 128
DEPTH = 1
ALPHA = (2.0 * DEPTH) ** 0.25
LN_EPS = 1e-5
ATTN_SCALE = HEAD_DIM ** -0.5
LAMBDA_INIT = 0.8 - 0.6 * math.exp(-0.3 * 0)
LOG2E = math.log2(math.e)
ONES_ROWS = 16
COL_U = 2 * CONV_WIDTH
COL_Q = ATTN_WIDTH
COL_K = ATTN_WIDTH
COL_V = N_HEADS * V_DIM

V7X_VMEM_LIMIT_BYTES = 56 * 1024 * 1024
FF_CHUNK = 256
TOKEN_TILE = 1024
EPILOGUE_PIECES = 4
CONV_TILE = 256
CONV_ROWS = 64
CONV_LANES = 256
SUBLANES = 8
ATTN_TILE = 1024
TILES_PER_TRIP = 2
QUERY_TILES_PER_STEP = 8
MAX_EXPONENT_DRIFT = 64.0
PROBE_KEYS = 16
BIAS_BLOCK = 128


def _dot(a, b):
    return jnp.dot(a, b, preferred_element_type=F32)


def _layer_norm(z, g, b):
    mu = jnp.mean(z, axis=-1, keepdims=True)
    zc = z - mu
    var = jnp.mean(zc * zc, axis=-1, keepdims=True)
    return zc * lax.rsqrt(var + LN_EPS) * g + b


def _resident(shape):
    return pl.BlockSpec(shape, lambda *_: (0,) * len(shape), pipeline_mode=pl.Buffered(1))


def _params(*semantics):
    return pltpu.CompilerParams(dimension_semantics=semantics,
                                vmem_limit_bytes=V7X_VMEM_LIMIT_BYTES)


def _ffn_ln_kernel(x_ref, wgu_ref, wd_ref, g_ref, b_ref, o_ref, h_ref):
    x = x_ref[...]
    xb = x.astype(BF16)
    for c in range(D_FF // FF_CHUNK):
        sl = slice(c * FF_CHUNK, (c + 1) * FF_CHUNK)
        a = _dot(xb, wgu_ref[:, sl])
        u = _dot(xb, wgu_ref[:, D_FF + c * FF_CHUNK:D_FF + (c + 1) * FF_CHUNK])
        h_ref[:, sl] = (a * jax.nn.sigmoid(a) * u).astype(BF16)
    rows = x.shape[0] // EPILOGUE_PIECES
    for r in range(EPILOGUE_PIECES):
        rs = slice(r * rows, (r + 1) * rows)
        y = _dot(h_ref[rs, :], wd_ref[...])
        o_ref[rs, :] = _layer_norm(ALPHA * x[rs] + 0.5 * y, g_ref[...], b_ref[...])


def _ffn_ln(x, wgu, wd, g, b):
    t = x.shape[0]
    tm = min(TOKEN_TILE, t)
    row = pl.BlockSpec((tm, D_MODEL), lambda i: (i, 0))
    return pl.pallas_call(
        _ffn_ln_kernel,
        grid=(t // tm,),
        in_specs=[row, _resident((D_MODEL, 2 * D_FF)),
                  _resident((D_FF, D_MODEL)), _resident((1, D_MODEL)), _resident((1, D_MODEL))],
        out_specs=row,
        out_shape=jax.ShapeDtypeStruct((t, D_MODEL), F32),
        scratch_shapes=[pltpu.VMEM((tm, D_FF), BF16)],
        compiler_params=_params("parallel"),
        name="ffn_ln",
    )(x, wgu, wd, g, b)


def _in_proj_kernel(x_ref, w_ref, hg_ref, qt_ref, k_ref, vt_ref):
    xb = x_ref[...].astype(BF16)
    u = _dot(xb, w_ref[:, 0:COL_U])
    hg_ref[...] = u[:, :CONV_WIDTH] * jax.nn.sigmoid(u[:, CONV_WIDTH:])
    o = COL_U
    qt_ref[0] = (_dot(xb, w_ref[:, o:o + COL_Q]) * (ATTN_SCALE * LOG2E)).T.astype(BF16)
    o += COL_Q
    k_ref[...] = _dot(xb, w_ref[:, o:o + COL_K]).astype(BF16)
    o += COL_K
    vt_ref[0] = _dot(xb, w_ref[:, o:o + COL_V]).T.astype(BF16)


def _in_proj(x1, w_uqkv, batch, seq):
    t = x1.shape[0]
    tm = min(TOKEN_TILE, seq)
    per_seq = seq // tm
    ncol = COL_U + COL_Q + COL_K + COL_V
    return pl.pallas_call(
        _in_proj_kernel,
        grid=(t // tm,),
        in_specs=[pl.BlockSpec((tm, D_MODEL), lambda i: (i, 0)), _resident((D_MODEL, ncol))],
        out_specs=[pl.BlockSpec((tm, CONV_WIDTH), lambda i: (i, 0)),
                   pl.BlockSpec((1, ATTN_WIDTH, tm), lambda i: (i // per_seq, 0, i % per_seq)),
                   pl.BlockSpec((tm, ATTN_WIDTH), lambda i: (i, 0)),
                   pl.BlockSpec((1, COL_V, tm), lambda i: (i // per_seq, 0, i % per_seq))],
        out_shape=[jax.ShapeDtypeStruct((t, CONV_WIDTH), F32),
                   jax.ShapeDtypeStruct((batch, ATTN_WIDTH, seq), BF16),
                   jax.ShapeDtypeStruct((t, ATTN_WIDTH), BF16),
                   jax.ShapeDtypeStruct((batch, COL_V, seq), BF16)],
        compiler_params=_params("parallel"),
        name="in_proj",
    )(x1, w_uqkv)


def _conv_kernel(prev_ref, cur_ref, next_ref, w_ref, b_ref, g_ref, beta_ref, o_ref, ext_ref, *, tc):
    i = pl.program_id(1)
    last = pl.num_programs(1) - 1
    ext_ref[0:CONV_HALO, :] = jnp.where(i > 0, prev_ref[0], 0.0)
    ext_ref[CONV_HALO:CONV_HALO + tc, :] = cur_ref[0]
    ext_ref[CONV_HALO + tc:2 * CONV_HALO + tc, :] = jnp.where(i < last, next_ref[0], 0.0)
    first_tap = CONV_HALO - CONV_KERNEL // 2
    for r in range(tc // CONV_ROWS):
        r0 = r * CONV_ROWS
        halves = []
        for c in range(CONV_WIDTH // CONV_LANES):
            cs = slice(c * CONV_LANES, (c + 1) * CONV_LANES)
            acc = None
            for b in range(SUBLANES):
                u = None
                for j in range(CONV_KERNEL):
                    if (first_tap + j) % SUBLANES != b:
                        continue
                    e0 = r0 + first_tap + j - b
                    rows = ext_ref[e0:e0 + CONV_ROWS + SUBLANES, cs].reshape(-1, SUBLANES, CONV_LANES)
                    term = rows * w_ref[j, :, cs]
                    u = term if u is None else u + term
                u = u.reshape(CONV_ROWS + SUBLANES, CONV_LANES)[b:b + CONV_ROWS]
                acc = u if acc is None else acc + u
            halves.append(acc)
        y = _layer_norm(jnp.concatenate(halves, axis=1) + b_ref[...], g_ref[...], beta_ref[...])
        o_ref[0, r0:r0 + CONV_ROWS, :] = (y * jax.nn.sigmoid(y)).astype(BF16)


def _conv_branch(hg, w_dw, b_dw, ln_g, ln_b, batch, seq):
    tc = min(CONV_TILE, seq)
    hpb = tc // CONV_HALO
    nhalo = seq // CONV_HALO
    hg3 = hg.reshape(batch, seq, CONV_WIDTH)
    out = pl.pallas_call(
        functools.partial(_conv_kernel, tc=tc),
        grid=(batch, seq // tc),
        in_specs=[
            pl.BlockSpec((1, CONV_HALO, CONV_WIDTH), lambda b, i: (b, jnp.maximum(i * hpb - 1, 0), 0)),
            pl.BlockSpec((1, tc, CONV_WIDTH), lambda b, i: (b, i, 0)),
            pl.BlockSpec((1, CONV_HALO, CONV_WIDTH),
                         lambda b, i: (b, jnp.minimum((i + 1) * hpb, nhalo - 1), 0)),
            _resident((CONV_KERNEL, SUBLANES, CONV_WIDTH)), _resident((1, CONV_WIDTH)),
            _resident((1, CONV_WIDTH)), _resident((1, CONV_WIDTH))],
        out_specs=pl.BlockSpec((1, tc, CONV_WIDTH), lambda b, i: (b, i, 0)),
        out_shape=jax.ShapeDtypeStruct((batch, seq, CONV_WIDTH), BF16),
        scratch_shapes=[pltpu.VMEM((tc + 2 * CONV_HALO, CONV_WIDTH), F32)],
        compiler_params=_params("parallel", "parallel"),
        name="conv_branch",
    )(hg3, hg3, hg3, w_dw, b_dw, ln_g, ln_b)
    return out.reshape(batch * seq, CONV_WIDTH)


def _t5_bucket(rel):
    nb = NUM_BUCKETS // 2
    ret = jnp.where(rel > 0, nb, 0)
    n = jnp.abs(rel)
    max_exact = nb // 2
    nf = jnp.maximum(n, 1).astype(F32)
    large = max_exact + (jnp.log(nf / max_exact) / math.log(MAX_DISTANCE / max_exact)
                         * (nb - max_exact)).astype(jnp.int32)
    large = jnp.minimum(large, nb - 1)
    return ret + jnp.where(n < max_exact, n, large)


def _bias_kernel(tab_ref, o_ref, *, tile):
    h = pl.program_id(0)
    d = pl.program_id(1)
    nb = tile // BIAS_BLOCK
    key = lax.broadcasted_iota(jnp.int32, (BIAS_BLOCK, BIAS_BLOCK), 0)
    query = lax.broadcasted_iota(jnp.int32, (BIAS_BLOCK, BIAS_BLOCK), 1)

    def banded_block(o):
        bucket = _t5_bucket(key - query + o * BIAS_BLOCK)
        acc = jnp.zeros((BIAS_BLOCK, BIAS_BLOCK), F32)
        for b in range(NUM_BUCKETS):
            acc = jnp.where(bucket == b, tab_ref[b, h], acc)
        return acc * LOG2E

    below, diag, above = banded_block(-1), banded_block(0), banded_block(1)
    far_below = tab_ref[NUM_BUCKETS // 2 - 1, h] * LOG2E
    far_above = tab_ref[NUM_BUCKETS - 1, h] * LOG2E
    for kb in range(nb):
        for qb in range(nb):
            o = kb - qb + nb * (d - 1)
            block = jnp.where(o == -1, below, jnp.where(o == 0, diag, above))
            block = jnp.where(o <= -2, far_below, jnp.where(o >= 2, far_above, block))
            o_ref[0, 0, kb * BIAS_BLOCK:(kb + 1) * BIAS_BLOCK, qb * BIAS_BLOCK:(qb + 1) * BIAS_BLOCK] = block


def _bias_tiles(table, tile):
    return pl.pallas_call(
        functools.partial(_bias_kernel, tile=tile),
        grid=(N_HEADS, 3),
        in_specs=[pl.BlockSpec(memory_space=pltpu.SMEM)],
        out_specs=pl.BlockSpec((1, 1, tile, tile), lambda h, d: (h, d, 0, 0)),
        out_shape=jax.ShapeDtypeStruct((N_HEADS, 3, tile, tile), F32),
        compiler_params=_params("parallel", "parallel"),
        name="bias_tiles",
    )(table)


def _attn_kernel(tab_ref, qt_ref, k_ref, vt_ref, bias_ref, lq1_ref, lk1_ref, lq2_ref, lk2_ref,
                 o_ref, acc_ref, m_ref, s_ref, drift_ref, *, tile, nk, q_tiles):
    h = pl.program_id(1)

    def query_tile(sub, exact):
        q0 = pl.multiple_of(sub * tile, tile)
        begin, rest, finish = _attn_query_tile(
            h, pl.program_id(2) * q_tiles + sub, qt_ref[0, :, pl.ds(q0, tile)], tab_ref, k_ref, vt_ref,
            bias_ref, lq1_ref, lk1_ref, lq2_ref, lk2_ref, acc_ref, m_ref, s_ref, drift_ref,
            tile=tile, nk=nk, exact=exact)

        def store():
            o_ref[0, :, pl.ds(q0, tile)] = finish()

        return begin, rest, store

    drift_ref[...] = jnp.zeros(drift_ref.shape, F32)
    query_tile(0, False)[0]()

    def body(sub, carry):
        _, rest, store = query_tile(sub, False)
        rest()
        store()
        query_tile(sub + 1, False)[0]()
        return carry

    lax.fori_loop(0, q_tiles - 1, body, 0)
    _, rest, store = query_tile(q_tiles - 1, False)
    rest()
    store()

    @pl.when(jnp.max(drift_ref[...]) > MAX_EXPONENT_DRIFT)
    def _():
        def redo(sub, carry):
            begin, rest, store = query_tile(sub, True)
            begin()
            rest()
            store()
            return carry

        lax.fori_loop(0, q_tiles, redo, 0)


def _attn_query_tile(h, qi, qt, tab_ref, k_ref, vt_ref, bias_ref, lq1_ref, lk1_ref, lq2_ref, lk2_ref,
                     acc_ref, m_ref, s_ref, drift_ref, *, tile, nk, exact):
    row = lax.broadcasted_iota(jnp.int32, qt.shape, 0)
    zero = jnp.zeros_like(qt)
    qt_maps = (jnp.where(row < HEAD_DIM, qt, zero), jnp.where(row >= HEAD_DIM, qt, zero))
    ones = jnp.ones((ONES_ROWS, tile), BF16)

    def keys(ki):
        k0 = pl.multiple_of(ki * tile, tile)
        vt_aug = jnp.concatenate([vt_ref[0, :, pl.ds(k0, tile)], ones], axis=0)
        return k_ref[pl.ds(k0, tile), :], vt_aug

    def logits(k, mi, bias):
        s = _dot(k, qt_maps[mi])
        return s if bias is None else s + bias

    def exact_update(stage, mi):
        ki, bias, shift = stage
        k, vt_aug = keys(ki)
        s_ref[...] = logits(k, mi, bias)
        m_old = m_ref[mi]
        m_new = jnp.maximum(m_old, jnp.max(s_ref[...], axis=0, keepdims=True) + shift)
        p = jnp.exp2(s_ref[...] - (m_new - shift)).astype(BF16)
        acc_ref[mi] = jnp.exp2(m_old - m_new) * acc_ref[mi] + _dot(vt_aug, p)
        m_ref[mi] = m_new

    def streamed_update(stage, mi, first=False):
        ki, bias, shift = stage
        k, vt_aug = keys(ki)
        s = logits(k, mi, bias)
        m_old = jnp.max(s[:PROBE_KEYS], axis=0, keepdims=True) + shift if first else m_ref[mi]
        p = jnp.exp2(s - (m_old - shift)).astype(BF16)
        t_max = jnp.max(s, axis=0, keepdims=True) + shift
        m_new = jnp.maximum(m_old, t_max)
        pv = _dot(vt_aug, p)
        acc_ref[mi] = jnp.exp2(m_old - m_new) * (pv if first else acc_ref[mi] + pv)
        m_ref[mi] = m_new
        drift_ref[mi] = jnp.maximum(drift_ref[mi], t_max - m_old)

    far_shift = lambda ki: LOG2E * jnp.where(ki < qi, tab_ref[NUM_BUCKETS // 2 - 1, h],
                                             tab_ref[NUM_BUCKETS - 1, h])

    lo = jnp.maximum(qi - 1, 0)
    hi = jnp.minimum(qi + 2, nk)

    def near_stage(ki):
        return ki, bias_ref[0, ki - qi + 1], 0.0

    def far_stage(j):
        ki = jnp.where(j < lo, j, j + (hi - lo))
        return ki, None, far_shift(ki)

    def sweep(update, stage_of, start, stop, tiles_per_trip=1):
        def trip(width):
            def body(i, carry):
                for u in range(width):
                    stage = stage_of(start + i * width + u)
                    for mi in range(2):
                        update(stage, mi)
                return carry
            return body

        full = (stop - start) // tiles_per_trip
        lax.fori_loop(0, full, trip(tiles_per_trip), 0)
        if tiles_per_trip > 1:
            done = start + full * tiles_per_trip
            lax.fori_loop(0, stop - done, lambda i, c: trip(1)(done - start + i, c), 0)

    n_far = nk - (hi - lo)

    def begin():
        if exact:
            m_ref[...] = jnp.full(m_ref.shape, -jnp.inf, F32)
            acc_ref[...] = jnp.zeros(acc_ref.shape, F32)
            sweep(exact_update, near_stage, lo, lo + 1)
        else:
            for mi in range(2):
                streamed_update(near_stage(lo), mi, first=True)

    def rest():
        if exact:
            sweep(exact_update, near_stage, lo + 1, hi)
            sweep(exact_update, far_stage, 0, n_far)
        else:
            sweep(streamed_update, near_stage, lo + 1, hi, TILES_PER_TRIP)
            sweep(streamed_update, far_stage, 0, n_far, TILES_PER_TRIP)

    def finish():
        lam = (jnp.exp(jnp.sum(lq1_ref[...] * lk1_ref[...], keepdims=True))
               - jnp.exp(jnp.sum(lq2_ref[...] * lk2_ref[...], keepdims=True)) + LAMBDA_INIT)
        o1 = acc_ref[0]
        o2 = acc_ref[1]
        return o1[:V_DIM] / o1[V_DIM:V_DIM + 1] - lam * (o2[:V_DIM] / o2[V_DIM:V_DIM + 1])

    return begin, rest, finish


def _attention(qt, k, vt, table, bias, lq1, lk1, lq2, lk2, batch, seq, tile):
    nk = seq // tile
    q_tiles = math.gcd(nk, QUERY_TILES_PER_STEP)
    assert seq % tile == 0 and tile % BIAS_BLOCK == 0 and BIAS_BLOCK >= MAX_DISTANCE
    small = _resident((1, HEAD_DIM))
    slab = pl.BlockSpec((1, V_DIM, q_tiles * tile), lambda b, h, i: (b, h, i))
    return pl.pallas_call(
        functools.partial(_attn_kernel, tile=tile, nk=nk, q_tiles=q_tiles),
        grid=(batch, N_HEADS, nk // q_tiles),
        in_specs=[
            pl.BlockSpec(memory_space=pltpu.SMEM),
            slab,
            pl.BlockSpec((seq, V_DIM), lambda b, h, i: (b, h), pipeline_mode=pl.Buffered(1)),
            pl.BlockSpec((1, V_DIM, seq), lambda b, h, i: (b, h, 0), pipeline_mode=pl.Buffered(1)),
            pl.BlockSpec((1, 3, tile, tile), lambda b, h, i: (h, 0, 0, 0), pipeline_mode=pl.Buffered(1)),
            small, small, small, small],
        out_specs=slab,
        out_shape=jax.ShapeDtypeStruct((batch, COL_V, seq), F32),
        scratch_shapes=[pltpu.VMEM((2, V_DIM + ONES_ROWS, tile), F32), pltpu.VMEM((2, 1, tile), F32),
                        pltpu.VMEM((tile, tile), F32), pltpu.VMEM((2, 1, tile), F32)],
        compiler_params=_params("parallel", "parallel", "parallel"),
        name="diff_attention",
    )(table, qt, k, vt, bias, lq1, lk1, lq2, lk2)


def _merge_ln_kernel(x_ref, hc_ref, at_ref, sg_ref, wco_ref, wao_ref, wg_ref, bg_ref, wo_ref, g_ref, b_ref,
                     o_ref):
    x = x_ref[...]
    xb = x.astype(BF16)
    y_conv = _dot(hc_ref[...], wco_ref[...])
    heads = []
    for hd in range(N_HEADS):
        a = at_ref[0, hd * V_DIM:(hd + 1) * V_DIM, :]
        r = a * lax.rsqrt(jnp.mean(a * a, axis=0, keepdims=True) + LN_EPS) * sg_ref[...]
        heads.append(r * (1.0 - LAMBDA_INIT))
    att = jnp.concatenate(heads, axis=0).T.astype(BF16)
    y_attn = _dot(att, wao_ref[...])
    g_conv = jax.nn.sigmoid(_dot(xb, wg_ref[:, :D_MODEL]) + bg_ref[:, :D_MODEL])
    g_attn = jax.nn.sigmoid(_dot(xb, wg_ref[:, D_MODEL:]) + bg_ref[:, D_MODEL:])
    merged = (g_conv * y_conv + g_attn * y_attn).astype(BF16)
    rows = x.shape[0] // EPILOGUE_PIECES
    for r in range(EPILOGUE_PIECES):
        rs = slice(r * rows, (r + 1) * rows)
        m = _dot(merged[rs], wo_ref[...])
        o_ref[rs, :] = _layer_norm(ALPHA * x[rs] + m, g_ref[...], b_ref[...])


def _merge_ln(x1, hc, att_t, subln_g, wco, wao, wg, bg, wo, g, b, seq):
    t = x1.shape[0]
    tm = min(TOKEN_TILE, seq)
    per_seq = seq // tm
    row = pl.BlockSpec((tm, D_MODEL), lambda i: (i, 0))
    half = pl.BlockSpec((tm, CONV_WIDTH), lambda i: (i, 0))
    return pl.pallas_call(
        _merge_ln_kernel,
        grid=(t // tm,),
        in_specs=[row, half, pl.BlockSpec((1, COL_V, tm), lambda i: (i // per_seq, 0, i % per_seq)),
                  _resident((V_DIM, 1)), _resident((CONV_WIDTH, D_MODEL)), _resident((ATTN_WIDTH, D_MODEL)),
                  _resident((D_MODEL, 2 * D_MODEL)), _resident((1, 2 * D_MODEL)),
                  _resident((D_MODEL, D_MODEL)), _resident((1, D_MODEL)), _resident((1, D_MODEL))],
        out_specs=row,
        out_shape=jax.ShapeDtypeStruct((t, D_MODEL), F32),
        compiler_params=_params("parallel"),
        name="merge_ln",
    )(x1, hc, att_t, subln_g, wco, wao, wg, bg, wo, g, b)


def _layer(x, p, bias, tile):
    batch, seq, _ = x.shape
    x0 = x.reshape(batch * seq, D_MODEL)
    x1 = _ffn_ln(x0, p["ffn1_wgu"], p["ffn1_wd"], p["ln1_g"], p["ln1_b"])
    hg, qt, k, vt = _in_proj(x1, p["w_uqkv"], batch, seq)
    hc = _conv_branch(hg, p["conv_w"], p["conv_b"], p["conv_ln_g"], p["conv_ln_b"], batch, seq)
    att_t = _attention(qt, k, vt, p["table"], bias, p["lq1"], p["lk1"], p["lq2"], p["lk2"], batch, seq, tile)
    x2 = _merge_ln(x1, hc, att_t, p["subln_g"], p["w_conv_out"], p["w_attn_out"], p["w_gate"], p["b_gate"],
                   p["w_o"], p["ln2_g"], p["ln2_b"], seq)
    x3 = _ffn_ln(x2, p["ffn2_wgu"], p["ffn2_wd"], p["ln3_g"], p["ln3_b"])
    return x3.reshape(batch, seq, D_MODEL)


def _prepare(rel_bias_table, ffn1_w_gu, ffn1_w_down, ln1_g, ln1_b, w_in, b_gate, conv_w_dw, conv_b_dw,
             conv_ln_g, conv_ln_b, w_conv_out, lambda_q1, lambda_k1, lambda_q2, lambda_k2, subln_g,
             w_attn_out, w_o, ln2_g, ln2_b, ffn2_w_gu, ffn2_w_down, ln3_g, ln3_b):
    l = 0
    n_uqkv = COL_U + COL_Q + COL_K + COL_V
    row = lambda a: a[l].reshape(1, -1).astype(F32)
    return {
        "table": rel_bias_table.astype(F32),
        "ffn1_wgu": ffn1_w_gu[l].astype(BF16),
        "ffn1_wd": ffn1_w_down[l].astype(BF16), "ln1_g": row(ln1_g), "ln1_b": row(ln1_b),
        "w_uqkv": w_in[l, :, :n_uqkv].astype(BF16), "w_gate": w_in[l, :, n_uqkv:].astype(BF16),
        "b_gate": row(b_gate),
        "conv_w": jnp.broadcast_to(conv_w_dw[l].reshape(CONV_KERNEL, 1, CONV_WIDTH).astype(F32),
                                   (CONV_KERNEL, SUBLANES, CONV_WIDTH)),
        "conv_b": row(conv_b_dw),
        "conv_ln_g": row(conv_ln_g), "conv_ln_b": row(conv_ln_b),
        "w_conv_out": w_conv_out[l].astype(BF16),
        "lq1": row(lambda_q1), "lk1": row(lambda_k1), "lq2": row(lambda_q2), "lk2": row(lambda_k2),
        "subln_g": subln_g[l].reshape(-1, 1).astype(F32), "w_attn_out": w_attn_out[l].astype(BF16), "w_o": w_o[l].astype(BF16),
        "ln2_g": row(ln2_g), "ln2_b": row(ln2_b),
        "ffn2_wgu": ffn2_w_gu[l].astype(BF16),
        "ffn2_wd": ffn2_w_down[l].astype(BF16), "ln3_g": row(ln3_g), "ln3_b": row(ln3_b),
    }


def kernel(x_prompt, x_sample, rel_bias_table, ffn1_w_gu, ffn1_w_down, ln1_g, ln1_b, w_in, b_gate, conv_w_dw, conv_b_dw, conv_ln_g, conv_ln_b, w_conv_out, lambda_q1, lambda_k1, lambda_q2, lambda_k2, subln_g, w_attn_out, w_o, ln2_g, ln2_b, ffn2_w_gu, ffn2_w_down, ln3_g, ln3_b):
    p = _prepare(rel_bias_table, ffn1_w_gu, ffn1_w_down, ln1_g, ln1_b, w_in, b_gate, conv_w_dw, conv_b_dw,
                 conv_ln_g, conv_ln_b, w_conv_out, lambda_q1, lambda_k1, lambda_q2, lambda_k2, subln_g,
                 w_attn_out, w_o, ln2_g, ln2_b, ffn2_w_gu, ffn2_w_down, ln3_g, ln3_b)
    tile = min(ATTN_TILE, x_prompt.shape[1], x_sample.shape[1])
    bias = _bias_tiles(p["table"], tile)
    return (_layer(x_prompt, p, bias, tile), _layer(x_sample, p, bias, tile))
```

```python
import functools
import math

import jax
import jax.numpy as jnp
from jax import lax
from jax.experimental import pallas as pl
from jax.experimental.pallas import tpu as pltpu

F32 = jnp.float32
BF16 = jnp.bfloat16

D_MODEL = 1024
N_HEADS = 4
HEAD_DIM = 64
V_DIM = 2 * HEAD_DIM
ATTN_WIDTH = N_HEADS * 2 * HEAD_DIM
CONV_WIDTH = 512
CONV_KERNEL = 31
CONV_HALO = 16
D_FF = 2816
NUM_BUCKETS = 32
MAX_DISTANCE = 128
DEPTH = 1
ALPHA = (2.0 * DEPTH) ** 0.25
LN_EPS = 1e-5
ATTN_SCALE = HEAD_DIM ** -0.5
LAMBDA_INIT = 0.8 - 0.6 * math.exp(-0.3 * 0)
LOG2E = math.log2(math.e)
ONES_ROWS = 16
COL_U = 2 * CONV_WIDTH
COL_Q = ATTN_WIDTH
COL_K = ATTN_WIDTH
COL_V = N_HEADS * V_DIM

V7X_VMEM_LIMIT_BYTES = 56 * 1024 * 1024
FF_CHUNK = 256
TOKEN_TILE = 1024
EPILOGUE_PIECES = 4
CONV_TILE = 256
CONV_ROWS = 64
CONV_LANES = 256
SUBLANES = 8
ATTN_TILE = 1024
TILES_PER_TRIP = 2
QUERY_HALVES = 2
QUERY_TILES_PER_STEP = 8
MAX_EXPONENT_DRIFT = 64.0
PROBE_KEYS = 16
BIAS_BLOCK = 128


def _dot(a, b):
    return jnp.dot(a, b, preferred_element_type=F32)


def _layer_norm(z, g, b):
    mu = jnp.mean(z, axis=-1, keepdims=True)
    zc = z - mu
    var = jnp.mean(zc * zc, axis=-1, keepdims=True)
    return zc * lax.rsqrt(var + LN_EPS) * g + b


def _resident(shape):
    return pl.BlockSpec(shape, lambda *_: (0,) * len(shape), pipeline_mode=pl.Buffered(1))


def _params(*semantics):
    return pltpu.CompilerParams(dimension_semantics=semantics,
                                vmem_limit_bytes=V7X_VMEM_LIMIT_BYTES)


def _ffn_ln_kernel(x_ref, wgu_ref, wd_ref, g_ref, b_ref, o_ref, h_ref):
    x = x_ref[...]
    xb = x.astype(BF16)
    for c in range(D_FF // FF_CHUNK):
        sl = slice(c * FF_CHUNK, (c + 1) * FF_CHUNK)
        a = _dot(xb, wgu_ref[:, sl])
        u = _dot(xb, wgu_ref[:, D_FF + c * FF_CHUNK:D_FF + (c + 1) * FF_CHUNK])
        h_ref[:, sl] = (a * jax.nn.sigmoid(a) * u).astype(BF16)
    rows = x.shape[0] // EPILOGUE_PIECES
    for r in range(EPILOGUE_PIECES):
        rs = slice(r * rows, (r + 1) * rows)
        y = _dot(h_ref[rs, :], wd_ref[...])
        o_ref[rs, :] = _layer_norm(ALPHA * x[rs] + 0.5 * y, g_ref[...], b_ref[...])


def _ffn_ln(x, wgu, wd, g, b):
    t = x.shape[0]
    tm = min(TOKEN_TILE, t)
    row = pl.BlockSpec((tm, D_MODEL), lambda i: (i, 0))
    return pl.pallas_call(
        _ffn_ln_kernel,
        grid=(t // tm,),
        in_specs=[row, _resident((D_MODEL, 2 * D_FF)),
                  _resident((D_FF, D_MODEL)), _resident((1, D_MODEL)), _resident((1, D_MODEL))],
        out_specs=row,
        out_shape=jax.ShapeDtypeStruct((t, D_MODEL), F32),
        scratch_shapes=[pltpu.VMEM((tm, D_FF), BF16)],
        compiler_params=_params("parallel"),
        name="ffn_ln",
    )(x, wgu, wd, g, b)


def _in_proj_kernel(x_ref, w_ref, hg_ref, qt_ref, k_ref, vt_ref):
    xb = x_ref[...].astype(BF16)
    u = _dot(xb, w_ref[:, 0:COL_U])
    hg_ref[...] = u[:, :CONV_WIDTH] * jax.nn.sigmoid(u[:, CONV_WIDTH:])
    o = COL_U
    qt_ref[0] = (_dot(xb, w_ref[:, o:o + COL_Q]) * (ATTN_SCALE * LOG2E)).T.astype(BF16)
    o += COL_Q
    k_ref[...] = _dot(xb, w_ref[:, o:o + COL_K]).astype(BF16)
    o += COL_K
    vt_ref[0] = _dot(xb, w_ref[:, o:o + COL_V]).T.astype(BF16)


def _in_proj(x1, w_uqkv, batch, seq):
    t = x1.shape[0]
    tm = min(TOKEN_TILE, seq)
    per_seq = seq // tm
    ncol = COL_U + COL_Q + COL_K + COL_V
    return pl.pallas_call(
        _in_proj_kernel,
        grid=(t // tm,),
        in_specs=[pl.BlockSpec((tm, D_MODEL), lambda i: (i, 0)), _resident((D_MODEL, ncol))],
        out_specs=[pl.BlockSpec((tm, CONV_WIDTH), lambda i: (i, 0)),
                   pl.BlockSpec((1, ATTN_WIDTH, tm), lambda i: (i // per_seq, 0, i % per_seq)),
                   pl.BlockSpec((tm, ATTN_WIDTH), lambda i: (i, 0)),
                   pl.BlockSpec((1, COL_V, tm), lambda i: (i // per_seq, 0, i % per_seq))],
        out_shape=[jax.ShapeDtypeStruct((t, CONV_WIDTH), F32),
                   jax.ShapeDtypeStruct((batch, ATTN_WIDTH, seq), BF16),
                   jax.ShapeDtypeStruct((t, ATTN_WIDTH), BF16),
                   jax.ShapeDtypeStruct((batch, COL_V, seq), BF16)],
        compiler_params=_params("parallel"),
        name="in_proj",
    )(x1, w_uqkv)


def _conv_kernel(prev_ref, cur_ref, next_ref, w_ref, b_ref, g_ref, beta_ref, o_ref, ext_ref, *, tc):
    i = pl.program_id(1)
    last = pl.num_programs(1) - 1
    ext_ref[0:CONV_HALO, :] = jnp.where(i > 0, prev_ref[0], 0.0)
    ext_ref[CONV_HALO:CONV_HALO + tc, :] = cur_ref[0]
    ext_ref[CONV_HALO + tc:2 * CONV_HALO + tc, :] = jnp.where(i < last, next_ref[0], 0.0)
    first_tap = CONV_HALO - CONV_KERNEL // 2
    for r in range(tc // CONV_ROWS):
        r0 = r * CONV_ROWS
        halves = []
        for c in range(CONV_WIDTH // CONV_LANES):
            cs = slice(c * CONV_LANES, (c + 1) * CONV_LANES)
            acc = None
            for b in range(SUBLANES):
                u = None
                for j in range(CONV_KERNEL):
                    if (first_tap + j) % SUBLANES != b:
                        continue
                    e0 = r0 + first_tap + j - b
                    rows = ext_ref[e0:e0 + CONV_ROWS + SUBLANES, cs].reshape(-1, SUBLANES, CONV_LANES)
                    term = rows * w_ref[j, :, cs]
                    u = term if u is None else u + term
                u = u.reshape(CONV_ROWS + SUBLANES, CONV_LANES)[b:b + CONV_ROWS]
                acc = u if acc is None else acc + u
            halves.append(acc)
        y = _layer_norm(jnp.concatenate(halves, axis=1) + b_ref[...], g_ref[...], beta_ref[...])
        o_ref[0, r0:r0 + CONV_ROWS, :] = (y * jax.nn.sigmoid(y)).astype(BF16)


def _conv_branch(hg, w_dw, b_dw, ln_g, ln_b, batch, seq):
    tc = min(CONV_TILE, seq)
    hpb = tc // CONV_HALO
    nhalo = seq // CONV_HALO
    hg3 = hg.reshape(batch, seq, CONV_WIDTH)
    out = pl.pallas_call(
        functools.partial(_conv_kernel, tc=tc),
        grid=(batch, seq // tc),
        in_specs=[
            pl.BlockSpec((1, CONV_HALO, CONV_WIDTH), lambda b, i: (b, jnp.maximum(i * hpb - 1, 0), 0)),
            pl.BlockSpec((1, tc, CONV_WIDTH), lambda b, i: (b, i, 0)),
            pl.BlockSpec((1, CONV_HALO, CONV_WIDTH),
                         lambda b, i: (b, jnp.minimum((i + 1) * hpb, nhalo - 1), 0)),
            _resident((CONV_KERNEL, SUBLANES, CONV_WIDTH)), _resident((1, CONV_WIDTH)),
            _resident((1, CONV_WIDTH)), _resident((1, CONV_WIDTH))],
        out_specs=pl.BlockSpec((1, tc, CONV_WIDTH), lambda b, i: (b, i, 0)),
        out_shape=jax.ShapeDtypeStruct((batch, seq, CONV_WIDTH), BF16),
        scratch_shapes=[pltpu.VMEM((tc + 2 * CONV_HALO, CONV_WIDTH), F32)],
        compiler_params=_params("parallel", "parallel"),
        name="conv_branch",
    )(hg3, hg3, hg3, w_dw, b_dw, ln_g, ln_b)
    return out.reshape(batch * seq, CONV_WIDTH)


def _t5_bucket(rel):
    nb = NUM_BUCKETS // 2
    ret = jnp.where(rel > 0, nb, 0)
    n = jnp.abs(rel)
    max_exact = nb // 2
    nf = jnp.maximum(n, 1).astype(F32)
    large = max_exact + (jnp.log(nf / max_exact) / math.log(MAX_DISTANCE / max_exact)
                         * (nb - max_exact)).astype(jnp.int32)
    large = jnp.minimum(large, nb - 1)
    return ret + jnp.where(n < max_exact, n, large)


def _bias_kernel(tab_ref, o_ref, *, tile):
    h = pl.program_id(0)
    d = pl.program_id(1)
    nb = tile // BIAS_BLOCK
    key = lax.broadcasted_iota(jnp.int32, (BIAS_BLOCK, BIAS_BLOCK), 0)
    query = lax.broadcasted_iota(jnp.int32, (BIAS_BLOCK, BIAS_BLOCK), 1)

    def banded_block(o):
        bucket = _t5_bucket(key - query + o * BIAS_BLOCK)
        acc = jnp.zeros((BIAS_BLOCK, BIAS_BLOCK), F32)
        for b in range(NUM_BUCKETS):
            acc = jnp.where(bucket == b, tab_ref[b, h], acc)
        return acc * LOG2E

    below, diag, above = banded_block(-1), banded_block(0), banded_block(1)
    far_below = tab_ref[NUM_BUCKETS // 2 - 1, h] * LOG2E
    far_above = tab_ref[NUM_BUCKETS - 1, h] * LOG2E
    for kb in range(nb):
        for qb in range(nb):
            o = kb - qb + nb * (d - 1)
            block = jnp.where(o == -1, below, jnp.where(o == 0, diag, above))
            block = jnp.where(o <= -2, far_below, jnp.where(o >= 2, far_above, block))
            o_ref[0, 0, kb * BIAS_BLOCK:(kb + 1) * BIAS_BLOCK, qb * BIAS_BLOCK:(qb + 1) * BIAS_BLOCK] = block


def _bias_tiles(table, tile):
    return pl.pallas_call(
        functools.partial(_bias_kernel, tile=tile),
        grid=(N_HEADS, 3),
        in_specs=[pl.BlockSpec(memory_space=pltpu.SMEM)],
        out_specs=pl.BlockSpec((1, 1, tile, tile), lambda h, d: (h, d, 0, 0)),
        out_shape=jax.ShapeDtypeStruct((N_HEADS, 3, tile, tile), F32),
        compiler_params=_params("parallel", "parallel"),
        name="bias_tiles",
    )(table)


def _attn_kernel(tab_ref, qt_ref, k_ref, vt_ref, bias_ref, lq1_ref, lk1_ref, lq2_ref, lk2_ref,
                 o_ref, acc_ref, m_ref, s_ref, drift_ref, *, tile, nk, q_tiles):
    h = pl.program_id(1)

    def query_tile(sub, exact):
        q0 = pl.multiple_of(sub * tile, tile)
        begin, rest, finish = _attn_query_tile(
            h, pl.program_id(2) * q_tiles + sub, qt_ref[0, :, pl.ds(q0, tile)], tab_ref, k_ref, vt_ref,
            bias_ref, lq1_ref, lk1_ref, lq2_ref, lk2_ref, acc_ref, m_ref, s_ref, drift_ref,
            tile=tile, nk=nk, exact=exact)

        def store():
            o_ref[0, :, pl.ds(q0, tile)] = finish()

        return begin, rest, store

    drift_ref[...] = jnp.zeros(drift_ref.shape, F32)
    query_tile(0, False)[0]()

    def body(sub, carry):
        _, rest, store = query_tile(sub, False)
        rest()
        store()
        query_tile(sub + 1, False)[0]()
        return carry

    lax.fori_loop(0, q_tiles - 1, body, 0)
    _, rest, store = query_tile(q_tiles - 1, False)
    rest()
    store()

    @pl.when(jnp.max(drift_ref[...]) > MAX_EXPONENT_DRIFT)
    def _():
        def redo(sub, carry):
            begin, rest, store = query_tile(sub, True)
            begin()
            rest()
            store()
            return carry

        lax.fori_loop(0, q_tiles, redo, 0)


def _attn_query_tile(h, qi, qt, tab_ref, k_ref, vt_ref, bias_ref, lq1_ref, lk1_ref, lq2_ref, lk2_ref,
                     acc_ref, m_ref, s_ref, drift_ref, *, tile, nk, exact):
    row = lax.broadcasted_iota(jnp.int32, qt.shape, 0)
    zero = jnp.zeros_like(qt)
    qt_maps = (jnp.where(row < HEAD_DIM, qt, zero), jnp.where(row >= HEAD_DIM, qt, zero))
    ones = jnp.ones((ONES_ROWS, tile), BF16)

    def keys(ki):
        k0 = pl.multiple_of(ki * tile, tile)
        vt_aug = jnp.concatenate([vt_ref[0, :, pl.ds(k0, tile)], ones], axis=0)
        return k_ref[pl.ds(k0, tile), :], vt_aug

    def logits(k, mi, bias):
        s = _dot(k, qt_maps[mi])
        return s if bias is None else s + bias

    def exact_update(stage, mi):
        ki, bias, shift = stage
        k, vt_aug = keys(ki)
        s_ref[...] = logits(k, mi, bias)
        m_old = m_ref[mi]
        m_new = jnp.maximum(m_old, jnp.max(s_ref[...], axis=0, keepdims=True) + shift)
        p = jnp.exp2(s_ref[...] - (m_new - shift)).astype(BF16)
        acc_ref[mi] = jnp.exp2(m_old - m_new) * acc_ref[mi] + _dot(vt_aug, p)
        m_ref[mi] = m_new

    def streamed_update(stage, mi, first=False):
        ki, bias, shift = stage
        k, vt_aug = keys(ki)
        half = tile // QUERY_HALVES
        for qh in range(QUERY_HALVES):
            qs = slice(qh * half, (qh + 1) * half)
            s = _dot(k, qt_maps[mi][:, qs])
            if bias is not None:
                s = s + bias[:, qs]
            m_old = jnp.max(s[:PROBE_KEYS], axis=0, keepdims=True) + shift if first else m_ref[mi, :, qs]
            p = jnp.exp2(s - (m_old - shift)).astype(BF16)
            t_max = jnp.max(s, axis=0, keepdims=True) + shift
            m_new = jnp.maximum(m_old, t_max)
            pv = _dot(vt_aug, p)
            acc_ref[mi, :, qs] = jnp.exp2(m_old - m_new) * (pv if first else acc_ref[mi, :, qs] + pv)
            m_ref[mi, :, qs] = m_new
            drift_ref[mi, :, qs] = jnp.maximum(drift_ref[mi, :, qs], t_max - m_old)

    far_shift = lambda ki: LOG2E * jnp.where(ki < qi, tab_ref[NUM_BUCKETS // 2 - 1, h],
                                             tab_ref[NUM_BUCKETS - 1, h])

    lo = jnp.maximum(qi - 1, 0)
    hi = jnp.minimum(qi + 2, nk)

    def near_stage(ki):
        return ki, bias_ref[0, ki - qi + 1], 0.0

    def far_stage(j):
        ki = jnp.where(j < lo, j, j + (hi - lo))
        return ki, None, far_shift(ki)

    def sweep(update, stage_of, start, stop, tiles_per_trip=1):
        def trip(width):
            def body(i, carry):
                for u in range(width):
                    stage = stage_of(start + i * width + u)
                    for mi in range(2):
                        update(stage, mi)
                return carry
            return body

        full = (stop - start) // tiles_per_trip
        lax.fori_loop(0, full, trip(tiles_per_trip), 0)
        if tiles_per_trip > 1:
            done = start + full * tiles_per_trip
            lax.fori_loop(0, stop - done, lambda i, c: trip(1)(done - start + i, c), 0)

    n_far = nk - (hi - lo)

    def begin():
        if exact:
            m_ref[...] = jnp.full(m_ref.shape, -jnp.inf, F32)
            acc_ref[...] = jnp.zeros(acc_ref.shape, F32)
            sweep(exact_update, near_stage, lo, lo + 1)
        else:
            for mi in range(2):
                streamed_update(near_stage(lo), mi, first=True)

    def rest():
        if exact:
            sweep(exact_update, near_stage, lo + 1, hi)
            sweep(exact_update, far_stage, 0, n_far)
        else:
            sweep(streamed_update, near_stage, lo + 1, hi, TILES_PER_TRIP)
            sweep(streamed_update, far_stage, 0, n_far, TILES_PER_TRIP)

    def finish():
        lam = (jnp.exp(jnp.sum(lq1_ref[...] * lk1_ref[...], keepdims=True))
               - jnp.exp(jnp.sum(lq2_ref[...] * lk2_ref[...], keepdims=True)) + LAMBDA_INIT)
        o1 = acc_ref[0]
        o2 = acc_ref[1]
        return o1[:V_DIM] / o1[V_DIM:V_DIM + 1] - lam * (o2[:V_DIM] / o2[V_DIM:V_DIM + 1])

    return begin, rest, finish


def _attention(qt, k, vt, table, bias, lq1, lk1, lq2, lk2, batch, seq, tile):
    nk = seq // tile
    q_tiles = math.gcd(nk, QUERY_TILES_PER_STEP)
    assert seq % tile == 0 and tile % BIAS_BLOCK == 0 and BIAS_BLOCK >= MAX_DISTANCE
    small = _resident((1, HEAD_DIM))
    slab = pl.BlockSpec((1, V_DIM, q_tiles * tile), lambda b, h, i: (b, h, i))
    return pl.pallas_call(
        functools.partial(_attn_kernel, tile=tile, nk=nk, q_tiles=q_tiles),
        grid=(batch, N_HEADS, nk // q_tiles),
        in_specs=[
            pl.BlockSpec(memory_space=pltpu.SMEM),
            slab,
            pl.BlockSpec((seq, V_DIM), lambda b, h, i: (b, h), pipeline_mode=pl.Buffered(1)),
            pl.BlockSpec((1, V_DIM, seq), lambda b, h, i: (b, h, 0), pipeline_mode=pl.Buffered(1)),
            pl.BlockSpec((1, 3, tile, tile), lambda b, h, i: (h, 0, 0, 0), pipeline_mode=pl.Buffered(1)),
            small, small, small, small],
        out_specs=slab,
        out_shape=jax.ShapeDtypeStruct((batch, COL_V, seq), F32),
        scratch_shapes=[pltpu.VMEM((2, V_DIM + ONES_ROWS, tile), F32), pltpu.VMEM((2, 1, tile), F32),
                        pltpu.VMEM((tile, tile), F32), pltpu.VMEM((2, 1, tile), F32)],
        compiler_params=_params("parallel", "parallel", "parallel"),
        name="diff_attention",
    )(table, qt, k, vt, bias, lq1, lk1, lq2, lk2)


def _merge_ln_kernel(x_ref, hc_ref, at_ref, sg_ref, wco_ref, wao_ref, wg_ref, bg_ref, wo_ref, g_ref, b_ref,
                     o_ref):
    x = x_ref[...]
    xb = x.astype(BF16)
    y_conv = _dot(hc_ref[...], wco_ref[...])
    heads = []
    for hd in range(N_HEADS):
        a = at_ref[0, hd * V_DIM:(hd + 1) * V_DIM, :]
        r = a * lax.rsqrt(jnp.mean(a * a, axis=0, keepdims=True) + LN_EPS) * sg_ref[...]
        heads.append(r * (1.0 - LAMBDA_INIT))
    att = jnp.concatenate(heads, axis=0).T.astype(BF16)
    y_attn = _dot(att, wao_ref[...])
    g_conv = jax.nn.sigmoid(_dot(xb, wg_ref[:, :D_MODEL]) + bg_ref[:, :D_MODEL])
    g_attn = jax.nn.sigmoid(_dot(xb, wg_ref[:, D_MODEL:]) + bg_ref[:, D_MODEL:])
    merged = (g_conv * y_conv + g_attn * y_attn).astype(BF16)
    rows = x.shape[0] // EPILOGUE_PIECES
    for r in range(EPILOGUE_PIECES):
        rs = slice(r * rows, (r + 1) * rows)
        m = _dot(merged[rs], wo_ref[...])
        o_ref[rs, :] = _layer_norm(ALPHA * x[rs] + m, g_ref[...], b_ref[...])


def _merge_ln(x1, hc, att_t, subln_g, wco, wao, wg, bg, wo, g, b, seq):
    t = x1.shape[0]
    tm = min(TOKEN_TILE, seq)
    per_seq = seq // tm
    row = pl.BlockSpec((tm, D_MODEL), lambda i: (i, 0))
    half = pl.BlockSpec((tm, CONV_WIDTH), lambda i: (i, 0))
    return pl.pallas_call(
        _merge_ln_kernel,
        grid=(t // tm,),
        in_specs=[row, half, pl.BlockSpec((1, COL_V, tm), lambda i: (i // per_seq, 0, i % per_seq)),
                  _resident((V_DIM, 1)), _resident((CONV_WIDTH, D_MODEL)), _resident((ATTN_WIDTH, D_MODEL)),
                  _resident((D_MODEL, 2 * D_MODEL)), _resident((1, 2 * D_MODEL)),
                  _resident((D_MODEL, D_MODEL)), _resident((1, D_MODEL)), _resident((1, D_MODEL))],
        out_specs=row,
        out_shape=jax.ShapeDtypeStruct((t, D_MODEL), F32),
        compiler_params=_params("parallel"),
        name="merge_ln",
    )(x1, hc, att_t, subln_g, wco, wao, wg, bg, wo, g, b)


def _layer(x, p, bias, tile):
    batch, seq, _ = x.shape
    x0 = x.reshape(batch * seq, D_MODEL)
    x1 = _ffn_ln(x0, p["ffn1_wgu"], p["ffn1_wd"], p["ln1_g"], p["ln1_b"])
    hg, qt, k, vt = _in_proj(x1, p["w_uqkv"], batch, seq)
    hc = _conv_branch(hg, p["conv_w"], p["conv_b"], p["conv_ln_g"], p["conv_ln_b"], batch, seq)
    att_t = _attention(qt, k, vt, p["table"], bias, p["lq1"], p["lk1"], p["lq2"], p["lk2"], batch, seq, tile)
    x2 = _merge_ln(x1, hc, att_t, p["subln_g"], p["w_conv_out"], p["w_attn_out"], p["w_gate"], p["b_gate"],
                   p["w_o"], p["ln2_g"], p["ln2_b"], seq)
    x3 = _ffn_ln(x2, p["ffn2_wgu"], p["ffn2_wd"], p["ln3_g"], p["ln3_b"])
    return x3.reshape(batch, seq, D_MODEL)


def _prepare(rel_bias_table, ffn1_w_gu, ffn1_w_down, ln1_g, ln1_b, w_in, b_gate, conv_w_dw, conv_b_dw,
             conv_ln_g, conv_ln_b, w_conv_out, lambda_q1, lambda_k1, lambda_q2, lambda_k2, subln_g,
             w_attn_out, w_o, ln2_g, ln2_b, ffn2_w_gu, ffn2_w_down, ln3_g, ln3_b):
    l = 0
    n_uqkv = COL_U + COL_Q + COL_K + COL_V
    row = lambda a: a[l].reshape(1, -1).astype(F32)
    return {
        "table": rel_bias_table.astype(F32),
        "ffn1_wgu": ffn1_w_gu[l].astype(BF16),
        "ffn1_wd": ffn1_w_down[l].astype(BF16), "ln1_g": row(ln1_g), "ln1_b": row(ln1_b),
        "w_uqkv": w_in[l, :, :n_uqkv].astype(BF16), "w_gate": w_in[l, :, n_uqkv:].astype(BF16),
        "b_gate": row(b_gate),
        "conv_w": jnp.broadcast_to(conv_w_dw[l].reshape(CONV_KERNEL, 1, CONV_WIDTH).astype(F32),
                                   (CONV_KERNEL, SUBLANES, CONV_WIDTH)),
        "conv_b": row(conv_b_dw),
        "conv_ln_g": row(conv_ln_g), "conv_ln_b": row(conv_ln_b),
        "w_conv_out": w_conv_out[l].astype(BF16),
        "lq1": row(lambda_q1), "lk1": row(lambda_k1), "lq2": row(lambda_q2), "lk2": row(lambda_k2),
        "subln_g": subln_g[l].reshape(-1, 1).astype(F32), "w_attn_out": w_attn_out[l].astype(BF16), "w_o": w_o[l].astype(BF16),
        "ln2_g": row(ln2_g), "ln2_b": row(ln2_b),
        "ffn2_wgu": ffn2_w_gu[l].astype(BF16),
        "ffn2_wd": ffn2_w_down[l].astype(BF16), "ln3_g": row(ln3_g), "ln3_b": row(ln3_b),
    }


def kernel(x_prompt, x_sample, rel_bias_table, ffn1_w_gu, ffn1_w_down, ln1_g, ln1_b, w_in, b_gate, conv_w_dw, conv_b_dw, conv_ln_g, conv_ln_b, w_conv_out, lambda_q1, lambda_k1, lambda_q2, lambda_k2, subln_g, w_attn_out, w_o, ln2_g, ln2_b, ffn2_w_gu, ffn2_w_down, ln3_g, ln3_b):
    p = _prepare(rel_bias_table, ffn1_w_gu, ffn1_w_down, ln1_g, ln1_b, w_in, b_gate, conv_w_dw, conv_b_dw,
                 conv_ln_g, conv_ln_b, w_conv_out, lambda_q1, lambda_k1, lambda_q2, lambda_k2, subln_g,
                 w_attn_out, w_o, ln2_g, ln2_b, ffn2_w_gu, ffn2_w_down, ln3_g, ln3_b)
    tile = min(ATTN_TILE, x_prompt.shape[1], x_sample.shape[1])
    bias = _bias_tiles(p["table"], tile)
    return (_layer(x_prompt, p, bias, tile), _layer(x_sample, p, bias, tile))
```
